```python
import math
import jax, jax.numpy as jnp
from jax import lax
import numpy as np

D_MODEL = 1024
BATCH = 16
SEQ = 4096
DEPTH = 4

PLE_DIM = 256
D_FF = 2816
MIX_WIDTH = D_MODEL
SSM_WIDTH = MIX_WIDTH // 2
POOL_WIDTH = MIX_WIDTH - SSM_WIDTH
SSM_GROUP_CH = 16
SSM_GROUPS = SSM_WIDTH // SSM_GROUP_CH
SSM_STATE = 64
POOL_WINDOWS = (2, 4, 8, 16)
POOL_GROUP_CH = POOL_WIDTH // len(POOL_WINDOWS)
EPS = 1e-6
DT_MIN = 1e-3
DT_MAX = 1e-1

kernel_name = "hybrid_s5_pool_macaron_ple"


def rms_norm(x, g):
    x32 = x.astype(jnp.float32)
    y = x32 * lax.rsqrt(jnp.mean(x32 * x32, axis=-1, keepdims=True) + EPS)
    return (y * g.astype(jnp.float32)).astype(x.dtype)


def swiglu(x, wi, wo):
    gu = x @ wi
    g, u = jnp.split(gu, 2, axis=-1)
    return (jax.nn.silu(g) * u) @ wo


def _ssm_combine(e1, e2):
    a1, b1 = e1
    a2, b2 = e2
    return a1 * a2, a2 * b1 + b2


def s5_mixer(u, lam_re, lam_im, log_dt, b_re, b_im, c_re, c_im, d_skip, w_glu):
    bsz, seq, _ = u.shape
    f32 = jnp.float32
    u32 = u.astype(f32)
    ug = u32.reshape(bsz, seq, SSM_GROUPS, SSM_GROUP_CH)
    lam = lax.complex(lam_re.astype(f32), lam_im.astype(f32))
    dt = jnp.exp(log_dt.astype(f32))[:, None]
    lam_bar = jnp.exp(lam * dt)
    b = lax.complex(b_re.astype(f32), b_im.astype(f32))
    b_bar = ((lam_bar - 1.0) / lam)[..., None] * b
    bu = jnp.einsum('blgh,gph->blgp', ug.astype(jnp.complex64), b_bar)
    a = jnp.broadcast_to(lam_bar[None, None], (1, seq, SSM_GROUPS, SSM_STATE))
    _, states = lax.associative_scan(_ssm_combine, (a, bu), axis=1)
    c = lax.complex(c_re.astype(f32), c_im.astype(f32))
    y = jnp.real(jnp.einsum('blgp,ghp->blgh', states, c)).reshape(bsz, seq, SSM_WIDTH)
    y = y + d_skip.astype(f32) * u32
    y = jax.nn.gelu(y)
    y = y * jax.nn.sigmoid(y @ w_glu.astype(f32))
    return y.astype(u.dtype)


def pool_mixer(u, w_pool, scale):
    bsz, seq, _ = u.shape
    u32 = u.astype(jnp.float32)
    cs = lax.cumsum(u32, axis=1)
    count = jnp.arange(1, seq + 1, dtype=jnp.float32)[:, None]
    outs = []
    for gi, win in enumerate(POOL_WINDOWS):
        sl = slice(gi * POOL_GROUP_CH, (gi + 1) * POOL_GROUP_CH)
        cg = cs[..., sl]
        prev = jnp.pad(cg, ((0, 0), (win, 0), (0, 0)))[:, :seq]
        mean = (cg - prev) / jnp.minimum(count, float(win))
        outs.append((mean - u32[..., sl]) @ w_pool[gi].astype(jnp.float32))
    y = jnp.concatenate(outs, axis=-1) * scale.astype(jnp.float32)
    return y.astype(u.dtype)


def setup_inputs(seed: int = 0) -> dict:
    key = jax.random.key(seed)
    ks = jax.random.split(key, 32)
    f32 = jnp.float32
    nrm = lambda k, shape, s: jax.random.normal(k, shape, f32) * s
    n_idx = jnp.arange(SSM_STATE, dtype=f32)
    lam_re = -0.5 + nrm(ks[0], (DEPTH, SSM_GROUPS, SSM_STATE), 0.01)
    lam_im = math.pi * n_idx[None, None, :] + nrm(ks[1], (DEPTH, SSM_GROUPS, SSM_STATE), 0.01)
    log_dt = jax.random.uniform(ks[2], (DEPTH, SSM_GROUPS), f32, math.log(DT_MIN), math.log(DT_MAX))
    return {
        "x": nrm(ks[3], (BATCH, SEQ, D_MODEL), 1.0),
        "p": nrm(ks[4], (DEPTH, BATCH, SEQ, PLE_DIM), 1.0),
        "ffn1_norm": 1.0 + nrm(ks[5], (DEPTH, D_MODEL), 0.02),
        "ffn1_wi": nrm(ks[6], (DEPTH, D_MODEL, 2 * D_FF), D_MODEL ** -0.5),
        "ffn1_wo": nrm(ks[7], (DEPTH, D_FF, D_MODEL), D_FF ** -0.5),
        "mix_norm": 1.0 + nrm(ks[8], (DEPTH, D_MODEL), 0.02),
        "w_in": nrm(ks[9], (DEPTH, D_MODEL, MIX_WIDTH), D_MODEL ** -0.5),
        "ssm_lambda_re": lam_re,
        "ssm_lambda_im": lam_im,
        "ssm_log_dt": log_dt,
        "ssm_b_re": nrm(ks[10], (DEPTH, SSM_GROUPS, SSM_STATE, SSM_GROUP_CH), (2.0 * SSM_GROUP_CH) ** -0.5),
        "ssm_b_im": nrm(ks[11], (DEPTH, SSM_GROUPS, SSM_STATE, SSM_GROUP_CH), (2.0 * SSM_GROUP_CH) ** -0.5),
        "ssm_c_re": nrm(ks[12], (DEPTH, SSM_GROUPS, SSM_GROUP_CH, SSM_STATE), (2.0 * SSM_STATE) ** -0.5),
        "ssm_c_im": nrm(ks[13], (DEPTH, SSM_GROUPS, SSM_GROUP_CH, SSM_STATE), (2.0 * SSM_STATE) ** -0.5),
        "ssm_d": nrm(ks[14], (DEPTH, SSM_WIDTH), 1.0),
        "ssm_w_glu": nrm(ks[15], (DEPTH, SSM_WIDTH, SSM_WIDTH), SSM_WIDTH ** -0.5),
        "pool_w": nrm(ks[16], (DEPTH, len(POOL_WINDOWS), POOL_GROUP_CH, POOL_GROUP_CH), POOL_GROUP_CH ** -0.5),
        "pool_scale": 1.0 + nrm(ks[17], (DEPTH, POOL_WIDTH), 0.02),
        "w_out": nrm(ks[18], (DEPTH, MIX_WIDTH, D_MODEL), MIX_WIDTH ** -0.5),
        "ffn2_norm": 1.0 + nrm(ks[19], (DEPTH, D_MODEL), 0.02),
        "ffn2_wi": nrm(ks[20], (DEPTH, D_MODEL, 2 * D_FF), D_MODEL ** -0.5),
        "ffn2_wo": nrm(ks[21], (DEPTH, D_FF, D_MODEL), D_FF ** -0.5),
        "ple_norm": 1.0 + nrm(ks[22], (DEPTH, D_MODEL), 0.02),
        "ple_w_gate": nrm(ks[23], (DEPTH, D_MODEL, D_MODEL), D_MODEL ** -0.5),
        "ple_w_proj": nrm(ks[24], (DEPTH, PLE_DIM, D_MODEL), PLE_DIM ** -0.5),
        "final_norm": 1.0 + nrm(ks[25], (D_MODEL,), 0.02),
    }


def reference(x, p, ffn1_norm, ffn1_wi, ffn1_wo, mix_norm, w_in,
              ssm_lambda_re, ssm_lambda_im, ssm_log_dt, ssm_b_re, ssm_b_im, ssm_c_re, ssm_c_im,
              ssm_d, ssm_w_glu, pool_w, pool_scale, w_out,
              ffn2_norm, ffn2_wi, ffn2_wo, ple_norm, ple_w_gate, ple_w_proj, final_norm):
    h = x
    for i in range(DEPTH):
        h = h + 0.5 * swiglu(rms_norm(h, ffn1_norm[i]), ffn1_wi[i], ffn1_wo[i])
        z = rms_norm(h, mix_norm[i]) @ w_in[i]
        y_ssm = s5_mixer(z[..., :SSM_WIDTH], ssm_lambda_re[i], ssm_lambda_im[i], ssm_log_dt[i],
                         ssm_b_re[i], ssm_b_im[i], ssm_c_re[i], ssm_c_im[i], ssm_d[i], ssm_w_glu[i])
        y_pool = pool_mixer(z[..., SSM_WIDTH:], pool_w[i], pool_scale[i])
        h = h + jnp.concatenate([y_ssm, y_pool], axis=-1) @ w_out[i]
        h = h + 0.5 * swiglu(rms_norm(h, ffn2_norm[i]), ffn2_wi[i], ffn2_wo[i])
        gate = jax.nn.sigmoid((rms_norm(h, ple_norm[i]) @ ple_w_gate[i]).astype(jnp.float32))
        h = h + (gate * (p[i] @ ple_w_proj[i]).astype(jnp.float32)).astype(h.dtype)
    return rms_norm(h, final_norm)
```

```python
import functools
import math

import jax
import jax.numpy as jnp
from jax import lax
from jax.experimental import pallas as pl
from jax.experimental.pallas import tpu as pltpu

D_MODEL = 1024
D_FF = 2816
PLE_DIM = 256
SSM_WIDTH = 512
POOL_WIDTH = 512
SSM_GROUP_CH = 16
SSM_GROUPS = 32
SSM_STATE = 64
POOL_WINDOWS = (2, 4, 8, 16)
POOL_GROUP_CH = 128
EPS = 1e-6

LANES = 128
STATE_LANES = 2 * SSM_GROUPS * SSM_STATE
SLAB = 1024
N_SLABS = STATE_LANES // SLAB
TILE_PAIR = 2 * LANES
POOL_HIST_STEPS = 16

FFN_ROWS = 1024
FFN_CHUNK = 256
MIX_STEPS = 32
VMEM_LIMIT = 56 * 1024 * 1024

F32 = jnp.float32
BF16 = jnp.bfloat16


def _rms(x, g):
    ms = jnp.mean(x * x, axis=-1, keepdims=True)
    return x * lax.rsqrt(ms + EPS) * g


def _const_spec(shape):
    nd = len(shape)
    return pl.BlockSpec(shape, lambda i: (0,) * nd, pipeline_mode=pl.Buffered(1))


def _row_spec(rows, cols):
    return pl.BlockSpec((rows, cols), lambda i: (i, 0))


def _ffn_kernel(x_ref, g_ref, wi_ref, wo_ref, o_ref, xn_ref):
    x = x_ref[...]
    xn_ref[...] = _rms(x, g_ref[...]).astype(BF16)
    acc = None
    for k in range(D_FF // FFN_CHUNK):
        c0 = k * FFN_CHUNK
        xn = xn_ref[...]
        g = jnp.dot(xn, wi_ref[:, c0:c0 + FFN_CHUNK], preferred_element_type=F32)
        u = jnp.dot(xn, wi_ref[:, D_FF + c0:D_FF + c0 + FFN_CHUNK], preferred_element_type=F32)
        a = (g * jax.nn.sigmoid(g) * u).astype(BF16)
        part = jnp.dot(a, wo_ref[c0:c0 + FFN_CHUNK, :], preferred_element_type=F32)
        acc = part if acc is None else acc + part
    o_ref[...] = x + 0.5 * acc


def _ffn(h, g, wi, wo):
    n = h.shape[0]
    return pl.pallas_call(
        _ffn_kernel,
        grid=(n // FFN_ROWS,),
        in_specs=[
            _row_spec(FFN_ROWS, D_MODEL),
            _const_spec((1, D_MODEL)),
            _const_spec((D_MODEL, 2 * D_FF)),
            _const_spec((D_FF, D_MODEL)),
        ],
        out_specs=_row_spec(FFN_ROWS, D_MODEL),
        out_shape=jax.ShapeDtypeStruct((n, D_MODEL), F32),
        scratch_shapes=[pltpu.VMEM((FFN_ROWS, D_MODEL), BF16)],
        compiler_params=pltpu.CompilerParams(
            dimension_semantics=("arbitrary",), vmem_limit_bytes=VMEM_LIMIT),
        name="ffn",
    )(h, g, wi, wo)


def _gate_kernel(x_ref, p_ref, g_ref, wg_ref, wp_ref, fg_ref, o_ref, *, final):
    x = x_ref[...]
    xn = _rms(x, g_ref[...]).astype(BF16)
    gate = jax.nn.sigmoid(jnp.dot(xn, wg_ref[...], preferred_element_type=F32))
    proj = jnp.dot(p_ref[...].astype(BF16), wp_ref[...], preferred_element_type=F32)
    y = x + gate * proj
    if final:
        y = _rms(y, fg_ref[...])
    o_ref[...] = y


def _gate(h, p, g, wg, wp, fg, final):
    n = h.shape[0]
    return pl.pallas_call(
        functools.partial(_gate_kernel, final=final),
        grid=(n // FFN_ROWS,),
        in_specs=[
            _row_spec(FFN_ROWS, D_MODEL),
            _row_spec(FFN_ROWS, PLE_DIM),
            _const_spec((1, D_MODEL)),
            _const_spec((D_MODEL, D_MODEL)),
            _const_spec((PLE_DIM, D_MODEL)),
            _const_spec((1, D_MODEL)),
        ],
        out_specs=_row_spec(FFN_ROWS, D_MODEL),
        out_shape=jax.ShapeDtypeStruct((n, D_MODEL), F32),
        compiler_params=pltpu.CompilerParams(
            dimension_semantics=("arbitrary",), vmem_limit_bytes=VMEM_LIMIT),
        name="gate",
    )(h, p, g, wg, wp, fg)


def _mixer_kernel(h_ref, g_ref, win_ref, bmat_ref, lam_ref, cmat_ref, dskip_ref, wglu_ref,
                  wpool_ref, pscale_ref, wout_ref, o_ref,
                  bu_ref, st_ref, carry_ref, hist_ref, *, batch):
    rows = h_ref.shape[0]
    steps = rows // batch
    hist_rows = POOL_HIST_STEPS * batch
    chunk = pl.program_id(0)

    @pl.when(chunk == 0)
    def _():
        carry_ref[...] = jnp.zeros_like(carry_ref)
        hist_ref[...] = jnp.zeros_like(hist_ref)

    h = h_ref[...]
    z = jnp.dot(_rms(h, g_ref[...]).astype(BF16), win_ref[...], preferred_element_type=F32)
    u = z[:, :SSM_WIDTH]
    zp = z[:, SSM_WIDTH:]

    ub = u.astype(BF16)
    for q in range(N_SLABS):
        bu_ref[:, q * SLAB:(q + 1) * SLAB] = jnp.dot(
            ub[:, q * LANES:(q + 1) * LANES], bmat_ref[q], preferred_element_type=F32)

    for q in range(N_SLABS):
        n_pairs = SLAB // TILE_PAIR
        a_re = [lam_ref[:, q * SLAB + j * TILE_PAIR:q * SLAB + j * TILE_PAIR + LANES]
                for j in range(n_pairs)]
        a_im = [lam_ref[:, q * SLAB + j * TILE_PAIR + LANES:q * SLAB + (j + 1) * TILE_PAIR]
                for j in range(n_pairs)]
        init = carry_ref[:, q * SLAB:(q + 1) * SLAB]
        s = [init[:, i * LANES:(i + 1) * LANES] for i in range(2 * n_pairs)]
        for t in range(steps):
            r0 = t * batch
            bu = bu_ref[r0:r0 + batch, q * SLAB:(q + 1) * SLAB]
            new = []
            for j in range(n_pairs):
                s_re, s_im = s[2 * j], s[2 * j + 1]
                b_re = bu[:, j * TILE_PAIR:j * TILE_PAIR + LANES]
                b_im = bu[:, j * TILE_PAIR + LANES:(j + 1) * TILE_PAIR]
                new.append(a_re[j] * s_re - a_im[j] * s_im + b_re)
                new.append(a_re[j] * s_im + a_im[j] * s_re + b_im)
            s = new
            st_ref[r0:r0 + batch, q * SLAB:(q + 1) * SLAB] = jnp.concatenate(s, axis=1).astype(BF16)
        carry_ref[:, q * SLAB:(q + 1) * SLAB] = jnp.concatenate(s, axis=1)

    y = jnp.concatenate(
        [jnp.dot(st_ref[:, q * SLAB:(q + 1) * SLAB], cmat_ref[q], preferred_element_type=F32)
         for q in range(N_SLABS)], axis=1)
    y = y + dskip_ref[...] * u
    y = jax.nn.gelu(y, approximate=True)
    y = y * jax.nn.sigmoid(jnp.dot(y.astype(BF16), wglu_ref[...], preferred_element_type=F32))

    ext = jnp.concatenate([hist_ref[...], zp], axis=0)
    hist_ref[...] = zp[rows - hist_rows:, :]
    t_idx = chunk * steps + lax.shift_right_logical(
        lax.broadcasted_iota(jnp.int32, (rows, LANES), 0), int(math.log2(batch)))
    pooled = []
    for gi, win in enumerate(POOL_WINDOWS):
        col = ext[:, gi * LANES:(gi + 1) * LANES]
        off = 0
        span = 1
        while span < win:
            sh = span * batch
            col = col[sh:, :] + col[:-sh, :]
            off += sh
            span *= 2
        wsum = col[hist_rows - off:hist_rows - off + rows, :]
        count = jnp.minimum(t_idx + 1, win).astype(F32)
        pooled.append(wsum / count - zp[:, gi * LANES:(gi + 1) * LANES])
    pooled = jnp.concatenate(pooled, axis=1).astype(BF16)
    yp = jnp.concatenate(
        [jnp.dot(pooled[:, k * TILE_PAIR:(k + 1) * TILE_PAIR], wpool_ref[k],
                 preferred_element_type=F32) for k in range(2)], axis=1)
    yp = yp * pscale_ref[...]

    mix = jnp.concatenate([y, yp], axis=1).astype(BF16)
    o_ref[...] = h + jnp.dot(mix, wout_ref[...], preferred_element_type=F32)


def _mixer(h, g, win, bmat, lam, cmat, dskip, wglu, wpool, pscale, wout, batch):
    n = h.shape[0]
    rows = MIX_STEPS * batch
    return pl.pallas_call(
        functools.partial(_mixer_kernel, batch=batch),
        grid=(n // rows,),
        in_specs=[
            _row_spec(rows, D_MODEL),
            _const_spec((1, D_MODEL)),
            _const_spec((D_MODEL, D_MODEL)),
            _const_spec((N_SLABS, LANES, SLAB)),
            _const_spec((1, STATE_LANES)),
            _const_spec((N_SLABS, SLAB, LANES)),
            _const_spec((1, SSM_WIDTH)),
            _const_spec((SSM_WIDTH, SSM_WIDTH)),
            _const_spec((2, TILE_PAIR, TILE_PAIR)),
            _const_spec((1, POOL_WIDTH)),
            _const_spec((D_MODEL, D_MODEL)),
        ],
        out_specs=_row_spec(rows, D_MODEL),
        out_shape=jax.ShapeDtypeStruct((n, D_MODEL), F32),
        scratch_shapes=[
            pltpu.VMEM((rows, STATE_LANES), F32),
            pltpu.VMEM((rows, STATE_LANES), BF16),
            pltpu.VMEM((batch, STATE_LANES), F32),
            pltpu.VMEM((POOL_HIST_STEPS * batch, POOL_WIDTH), F32),
        ],
        compiler_params=pltpu.CompilerParams(
            dimension_semantics=("arbitrary",), vmem_limit_bytes=VMEM_LIMIT),
        name="mixer",
    )(h, g, win, bmat, lam, cmat, dskip, wglu, wpool, pscale, wout)


def _ssm_params(lam_re, lam_im, log_dt, b_re, b_im, c_re, c_im):
    dt = jnp.exp(log_dt)[:, None]
    mag = jnp.exp(lam_re * dt)
    ab_re = mag * jnp.cos(lam_im * dt)
    ab_im = mag * jnp.sin(lam_im * dt)
    den = lam_re * lam_re + lam_im * lam_im
    f_re = ((ab_re - 1.0) * lam_re + ab_im * lam_im) / den
    f_im = (ab_im * lam_re - (ab_re - 1.0) * lam_im) / den
    bb_re = f_re[..., None] * b_re - f_im[..., None] * b_im
    bb_im = f_re[..., None] * b_im + f_im[..., None] * b_re

    n_q = N_SLABS
    gl = SSM_GROUPS // n_q
    n_j = gl // 2
    delta = (jnp.arange(gl)[:, None, None] == 2 * jnp.arange(n_j)[None, :, None]
             + jnp.arange(2)[None, None, :]).astype(F32)

    lam_lanes = jnp.stack([ab_re.reshape(SSM_GROUPS // 2, 2, SSM_STATE),
                           ab_im.reshape(SSM_GROUPS // 2, 2, SSM_STATE)], axis=1)
    lam_lanes = lam_lanes.reshape(1, STATE_LANES)

    bb = jnp.stack([bb_re, bb_im], axis=0).reshape(2, n_q, gl, SSM_STATE, SSM_GROUP_CH)
    bb = bb.transpose(1, 2, 4, 0, 3)
    bmat = bb[:, :, :, None, :, None, :] * delta[None, :, None, :, None, :, None]
    bmat = bmat.reshape(n_q, gl * SSM_GROUP_CH, SLAB)

    cc = jnp.stack([c_re, -c_im], axis=0).reshape(2, n_q, gl, SSM_GROUP_CH, SSM_STATE)
    cc = cc.transpose(1, 0, 4, 2, 3)
    cmat = cc[:, None, :, None, :, :, :] * delta.transpose(1, 2, 0)[None, :, None, :, None, :, None]
    cmat = cmat.reshape(n_q, SLAB, gl * SSM_GROUP_CH)
    return lam_lanes, bmat.astype(BF16), cmat.astype(BF16)


def _pool_pairs(w_pool):
    zeros = jnp.zeros((POOL_GROUP_CH, POOL_GROUP_CH), w_pool.dtype)
    pairs = [jnp.block([[w_pool[2 * k], zeros], [zeros, w_pool[2 * k + 1]]]) for k in range(2)]
    return jnp.stack(pairs, axis=0).astype(BF16)


def kernel(x, p, ffn1_norm, ffn1_wi, ffn1_wo, mix_norm, w_in, ssm_lambda_re, ssm_lambda_im, ssm_log_dt, ssm_b_re, ssm_b_im, ssm_c_re, ssm_c_im, ssm_d, ssm_w_glu, pool_w, pool_scale, w_out, ffn2_norm, ffn2_wi, ffn2_wo, ple_norm, ple_w_gate, ple_w_proj, final_norm):
    batch, seq, d = x.shape
    depth = p.shape[0]
    n = batch * seq
    h = x.transpose(1, 0, 2).reshape(n, d)
    pt = p.transpose(0, 2, 1, 3).reshape(depth, n, PLE_DIM)
    row = lambda v: v.reshape(1, -1).astype(F32)
    for i in range(depth):
        h = _ffn(h, row(ffn1_norm[i]), ffn1_wi[i].astype(BF16), ffn1_wo[i].astype(BF16))
        lam, bmat, cmat = _ssm_params(ssm_lambda_re[i], ssm_lambda_im[i], ssm_log_dt[i],
                                      ssm_b_re[i], ssm_b_im[i], ssm_c_re[i], ssm_c_im[i])
        h = _mixer(h, row(mix_norm[i]), w_in[i].astype(BF16), bmat, lam, cmat, row(ssm_d[i]),
                   ssm_w_glu[i].astype(BF16), _pool_pairs(pool_w[i]), row(pool_scale[i]),
                   w_out[i].astype(BF16), batch)
        h = _ffn(h, row(ffn2_norm[i]), ffn2_wi[i].astype(BF16), ffn2_wo[i].astype(BF16))
        h = _gate(h, pt[i], row(ple_norm[i]), ple_w_gate[i].astype(BF16),
                  ple_w_proj[i].astype(BF16), row(final_norm), final=(i == depth - 1))
    return h.reshape(seq, batch, d).transpose(1, 0, 2)
```

```python
import functools
import math

import jax
import jax.numpy as jnp
from jax import lax
from jax.experimental import pallas as pl
from jax.experimental.pallas import tpu as pltpu

D_MODEL = 1024
D_FF = 2816
PLE_DIM = 256
SSM_WIDTH = 512
POOL_WIDTH = 512
SSM_GROUP_CH = 16
SSM_GROUPS = 32
SSM_STATE = 64
POOL_WINDOWS = (2, 4, 8, 16)
POOL_GROUP_CH = 128
EPS = 1e-6

LANES = 128
STATE_LANES = 2 * SSM_GROUPS * SSM_STATE
SLAB = 1024
N_SLABS = STATE_LANES // SLAB
TILE_PAIR = 2 * LANES
POOL_HIST_STEPS = 16

FFN_ROWS = 1024
FFN_CHUNK = 256
MIX_STEPS = 32
VMEM_LIMIT = 56 * 1024 * 1024

F32 = jnp.float32
BF16 = jnp.bfloat16


def _inv_rms(x):
    return lax.rsqrt(jnp.mean(x * x, axis=-1, keepdims=True) + EPS)


def _layer_spec(shape, layer):
    nd = len(shape)
    return pl.BlockSpec((None,) + shape, lambda i: (layer,) + (0,) * nd,
                        pipeline_mode=pl.Buffered(1))


def _const_spec(shape):
    nd = len(shape)
    return pl.BlockSpec(shape, lambda i: (0,) * nd, pipeline_mode=pl.Buffered(1))


def _row_spec(rows, cols):
    return pl.BlockSpec((rows, cols), lambda i: (i, 0))


def _ffn_kernel(*refs, gated, final):
    if gated:
        (x_ref, g_ref, wi_ref, wo_ref, p_ref, pg_ref, wg_ref, wp_ref, fg_ref, o_ref,
         xg_ref, a_ref) = refs
    else:
        (x_ref, g_ref, wi_ref, wo_ref, o_ref, xg_ref, a_ref) = refs
    x = x_ref[...]
    xg_ref[...] = (x * g_ref[...]).astype(BF16)
    r = _inv_rms(x)
    for k in range(D_FF // FFN_CHUNK):
        c0 = k * FFN_CHUNK
        xg = xg_ref[...]
        g = r * jnp.dot(xg, wi_ref[:, c0:c0 + FFN_CHUNK], preferred_element_type=F32)
        u = r * jnp.dot(xg, wi_ref[:, D_FF + c0:D_FF + c0 + FFN_CHUNK], preferred_element_type=F32)
        a_ref[:, c0:c0 + FFN_CHUNK] = (g * jax.nn.sigmoid(g) * u).astype(BF16)
    y = x + 0.5 * jnp.dot(a_ref[...], wo_ref[...], preferred_element_type=F32)
    if gated:
        gate = jax.nn.sigmoid(_inv_rms(y) * jnp.dot(
            (y * pg_ref[...]).astype(BF16), wg_ref[...], preferred_element_type=F32))
        proj = jnp.dot(p_ref[...].astype(BF16), wp_ref[...], preferred_element_type=F32)
        y = y + gate * proj
        if final:
            y = y * _inv_rms(y) * fg_ref[...]
    o_ref[...] = y


def _ffn(h, layer, g, wi, wo, gate_args=None, final=False):
    n = h.shape[0]
    in_specs = [
        _row_spec(FFN_ROWS, D_MODEL),
        _layer_spec((1, D_MODEL), layer),
        _layer_spec((D_MODEL, 2 * D_FF), layer),
        _layer_spec((D_FF, D_MODEL), layer),
    ]
    args = [h, g, wi, wo]
    if gate_args is not None:
        in_specs += [
            pl.BlockSpec((None, FFN_ROWS, PLE_DIM), lambda i: (layer, i, 0)),
            _layer_spec((1, D_MODEL), layer),
            _layer_spec((D_MODEL, D_MODEL), layer),
            _layer_spec((PLE_DIM, D_MODEL), layer),
            _const_spec((1, D_MODEL)),
        ]
        args += list(gate_args)
    return pl.pallas_call(
        functools.partial(_ffn_kernel, gated=gate_args is not None, final=final),
        grid=(n // FFN_ROWS,),
        in_specs=in_specs,
        out_specs=_row_spec(FFN_ROWS, D_MODEL),
        out_shape=jax.ShapeDtypeStruct((n, D_MODEL), F32),
        scratch_shapes=[pltpu.VMEM((FFN_ROWS, D_MODEL), BF16),
                        pltpu.VMEM((FFN_ROWS, D_FF), BF16)],
        compiler_params=pltpu.CompilerParams(
            dimension_semantics=("arbitrary",), vmem_limit_bytes=VMEM_LIMIT),
        name="ffn_gate" if gate_args is not None else "ffn",
    )(*args)


def _mixer_kernel(h_ref, g_ref, win_ref, bmat_ref, lam_ref, cmat_ref, kmat_ref, dskip_ref, wglu_ref,
                  wpool_ref, pscale_ref, wout_ref, o_ref,
                  bu_ref, st_ref, carry_ref, hist_ref, *, batch):
    rows = h_ref.shape[0]
    steps = rows // batch
    hist_rows = POOL_HIST_STEPS * batch
    chunk = pl.program_id(0)

    @pl.when(chunk == 0)
    def _():
        carry_ref[...] = jnp.zeros_like(carry_ref)
        hist_ref[...] = jnp.zeros_like(hist_ref)

    h = h_ref[...]
    z = _inv_rms(h) * jnp.dot((h * g_ref[...]).astype(BF16), win_ref[...],
                              preferred_element_type=F32)
    u = z[:, :SSM_WIDTH]
    zp = z[:, SSM_WIDTH:]

    pairs = steps // 2
    ub = u.astype(BF16)
    ub_even = jnp.concatenate([ub[2 * m * batch:(2 * m + 1) * batch] for m in range(pairs)], axis=0)
    ub_odd = jnp.concatenate([ub[(2 * m + 1) * batch:(2 * m + 2) * batch] for m in range(pairs)],
                             axis=0)
    u2 = [jnp.concatenate([ub_even[:, q * LANES:(q + 1) * LANES],
                           ub_odd[:, q * LANES:(q + 1) * LANES]], axis=1) for q in range(N_SLABS)]
    for q in range(N_SLABS):
        bu_ref[:, q * SLAB:(q + 1) * SLAB] = jnp.dot(u2[q], bmat_ref[q],
                                                    preferred_element_type=F32)

    for q in range(N_SLABS):
        n_tp = SLAB // TILE_PAIR
        a_re = [lam_ref[:, q * SLAB + j * TILE_PAIR:q * SLAB + j * TILE_PAIR + LANES]
                for j in range(n_tp)]
        a_im = [lam_ref[:, q * SLAB + j * TILE_PAIR + LANES:q * SLAB + (j + 1) * TILE_PAIR]
                for j in range(n_tp)]
        init = carry_ref[:, q * SLAB:(q + 1) * SLAB]
        s = [init[:, i * LANES:(i + 1) * LANES] for i in range(2 * n_tp)]
        for m in range(pairs):
            r0 = m * batch
            st_ref[r0:r0 + batch, q * SLAB:(q + 1) * SLAB] = jnp.concatenate(s, axis=1).astype(BF16)
            bu = bu_ref[r0:r0 + batch, q * SLAB:(q + 1) * SLAB]
            new = []
            for j in range(n_tp):
                s_re, s_im = s[2 * j], s[2 * j + 1]
                b_re = bu[:, j * TILE_PAIR:j * TILE_PAIR + LANES]
                b_im = bu[:, j * TILE_PAIR + LANES:(j + 1) * TILE_PAIR]
                new.append(a_re[j] * s_re - a_im[j] * s_im + b_re)
                new.append(a_re[j] * s_im + a_im[j] * s_re + b_im)
            s = new
        carry_ref[:, q * SLAB:(q + 1) * SLAB] = jnp.concatenate(s, axis=1)

    y2 = [jnp.dot(st_ref[:, q * SLAB:(q + 1) * SLAB], cmat_ref[q], preferred_element_type=F32)
          + jnp.dot(u2[q], kmat_ref[q], preferred_element_type=F32) for q in range(N_SLABS)]
    y_even = jnp.concatenate([y2[q][:, :LANES] for q in range(N_SLABS)], axis=1)
    y_odd = jnp.concatenate([y2[q][:, LANES:] for q in range(N_SLABS)], axis=1)
    y = jnp.concatenate(
        [blk[m * batch:(m + 1) * batch] for m in range(pairs) for blk in (y_even, y_odd)], axis=0)

    y = y + dskip_ref[...] * u
    y = jax.nn.gelu(y, approximate=True)
    y = y * jax.nn.sigmoid(jnp.dot(y.astype(BF16), wglu_ref[...], preferred_element_type=F32))

    ext = jnp.concatenate([hist_ref[...], zp], axis=0)
    hist_ref[...] = zp[rows - hist_rows:, :]
    t_idx = chunk * steps + lax.shift_right_logical(
        lax.broadcasted_iota(jnp.int32, (rows, LANES), 0), int(math.log2(batch)))
    pooled = []
    for gi, win in enumerate(POOL_WINDOWS):
        col = ext[:, gi * LANES:(gi + 1) * LANES]
        off = 0
        span = 1
        while span < win:
            sh = span * batch
            col = col[sh:, :] + col[:-sh, :]
            off += sh
            span *= 2
        wsum = col[hist_rows - off:hist_rows - off + rows, :]
        count = jnp.minimum(t_idx + 1, win).astype(F32)
        pooled.append(wsum / count - zp[:, gi * LANES:(gi + 1) * LANES])
    pooled = jnp.concatenate(pooled, axis=1).astype(BF16)
    yp = jnp.concatenate(
        [jnp.dot(pooled[:, k * TILE_PAIR:(k + 1) * TILE_PAIR], wpool_ref[k],
                 preferred_element_type=F32) for k in range(2)], axis=1)
    yp = yp * pscale_ref[...]

    mix = jnp.concatenate([y, yp], axis=1).astype(BF16)
    o_ref[...] = h + jnp.dot(mix, wout_ref[...], preferred_element_type=F32)


def _mixer(h, layer, g, win, bmat, lam, cmat, kmat, dskip, wglu, wpool, pscale, wout, batch):
    n = h.shape[0]
    rows = MIX_STEPS * batch
    return pl.pallas_call(
        functools.partial(_mixer_kernel, batch=batch),
        grid=(n // rows,),
        in_specs=[
            _row_spec(rows, D_MODEL),
            _layer_spec((1, D_MODEL), layer),
            _layer_spec((D_MODEL, D_MODEL), layer),
            _layer_spec((N_SLABS, TILE_PAIR, SLAB), layer),
            _layer_spec((1, STATE_LANES), layer),
            _layer_spec((N_SLABS, SLAB, TILE_PAIR), layer),
            _layer_spec((N_SLABS, TILE_PAIR, TILE_PAIR), layer),
            _layer_spec((1, SSM_WIDTH), layer),
            _layer_spec((SSM_WIDTH, SSM_WIDTH), layer),
            _layer_spec((2, TILE_PAIR, TILE_PAIR), layer),
            _layer_spec((1, POOL_WIDTH), layer),
            _layer_spec((D_MODEL, D_MODEL), layer),
        ],
        out_specs=_row_spec(rows, D_MODEL),
        out_shape=jax.ShapeDtypeStruct((n, D_MODEL), F32),
        scratch_shapes=[
            pltpu.VMEM((rows // 2, STATE_LANES), F32),
            pltpu.VMEM((rows // 2, STATE_LANES), BF16),
            pltpu.VMEM((batch, STATE_LANES), F32),
            pltpu.VMEM((POOL_HIST_STEPS * batch, POOL_WIDTH), F32),
        ],
        compiler_params=pltpu.CompilerParams(
            dimension_semantics=("arbitrary",), vmem_limit_bytes=VMEM_LIMIT),
        name="mixer",
    )(h, g, win, bmat, lam, cmat, kmat, dskip, wglu, wpool, pscale, wout)


def _ssm_params(lam_re, lam_im, log_dt, b_re, b_im, c_re, c_im):
    dt = jnp.exp(log_dt)[:, None]
    mag = jnp.exp(lam_re * dt)
    ab_re = mag * jnp.cos(lam_im * dt)
    ab_im = mag * jnp.sin(lam_im * dt)
    den = lam_re * lam_re + lam_im * lam_im
    f_re = ((ab_re - 1.0) * lam_re + ab_im * lam_im) / den
    f_im = (ab_im * lam_re - (ab_re - 1.0) * lam_im) / den
    bb_re = f_re[..., None] * b_re - f_im[..., None] * b_im
    bb_im = f_re[..., None] * b_im + f_im[..., None] * b_re

    n_q = N_SLABS
    gl = SSM_GROUPS // n_q
    n_j = gl // 2
    delta = (jnp.arange(gl)[:, None, None] == 2 * jnp.arange(n_j)[None, :, None]
             + jnp.arange(2)[None, None, :]).astype(F32)

    def embed_in(re, im):
        v = jnp.stack([re, im], axis=0).reshape(2, n_q, gl, SSM_STATE, SSM_GROUP_CH)
        v = v.transpose(1, 2, 4, 0, 3)
        v = v[:, :, :, None, :, None, :] * delta[None, :, None, :, None, :, None]
        return v.reshape(n_q, gl * SSM_GROUP_CH, SLAB)

    def embed_out(re, im):
        v = jnp.stack([re, -im], axis=0).reshape(2, n_q, gl, SSM_GROUP_CH, SSM_STATE)
        v = v.transpose(1, 0, 4, 2, 3)
        v = v[:, None, :, None, :, :, :] * delta.transpose(1, 2, 0)[None, :, None, :, None, :, None]
        return v.reshape(n_q, SLAB, gl * SSM_GROUP_CH)

    def embed_direct(k):
        v = k.reshape(n_q, gl, SSM_GROUP_CH, SSM_GROUP_CH)
        v = v[:, :, :, None, :] * jnp.eye(gl, dtype=F32)[None, :, None, :, None]
        return v.reshape(n_q, gl * SSM_GROUP_CH, gl * SSM_GROUP_CH)

    a2_re = ab_re * ab_re - ab_im * ab_im
    a2_im = 2.0 * ab_re * ab_im
    lb_re = ab_re[..., None] * bb_re - ab_im[..., None] * bb_im
    lb_im = ab_re[..., None] * bb_im + ab_im[..., None] * bb_re
    cl1_re = c_re * ab_re[:, None, :] - c_im * ab_im[:, None, :]
    cl1_im = c_re * ab_im[:, None, :] + c_im * ab_re[:, None, :]
    cl2_re = c_re * a2_re[:, None, :] - c_im * a2_im[:, None, :]
    cl2_im = c_re * a2_im[:, None, :] + c_im * a2_re[:, None, :]

    lam_lanes = jnp.stack([a2_re.reshape(SSM_GROUPS // 2, 2, SSM_STATE),
                           a2_im.reshape(SSM_GROUPS // 2, 2, SSM_STATE)], axis=1)
    lam_lanes = lam_lanes.reshape(1, STATE_LANES)

    bmat = jnp.concatenate([embed_in(lb_re, lb_im), embed_in(bb_re, bb_im)], axis=1)
    cmat = jnp.concatenate([embed_out(cl1_re, cl1_im), embed_out(cl2_re, cl2_im)], axis=2)
    ein = functools.partial(jnp.einsum, 'gop,gpi->gio', precision=lax.Precision.HIGHEST)
    k0 = embed_direct(ein(c_re, bb_re) - ein(c_im, bb_im))
    k1 = embed_direct(ein(c_re, lb_re) - ein(c_im, lb_im))
    kmat = jnp.concatenate([jnp.concatenate([k0, k1], axis=2),
                            jnp.concatenate([jnp.zeros_like(k0), k0], axis=2)], axis=1)
    return lam_lanes, bmat.astype(BF16), cmat.astype(BF16), kmat.astype(BF16)


def _pool_pairs(w_pool):
    zeros = jnp.zeros((POOL_GROUP_CH, POOL_GROUP_CH), w_pool.dtype)
    pairs = [jnp.block([[w_pool[2 * k], zeros], [zeros, w_pool[2 * k + 1]]]) for k in range(2)]
    return jnp.stack(pairs, axis=0).astype(BF16)


def kernel(x, p, ffn1_norm, ffn1_wi, ffn1_wo, mix_norm, w_in, ssm_lambda_re, ssm_lambda_im, ssm_log_dt, ssm_b_re, ssm_b_im, ssm_c_re, ssm_c_im, ssm_d, ssm_w_glu, pool_w, pool_scale, w_out, ffn2_norm, ffn2_wi, ffn2_wo, ple_norm, ple_w_gate, ple_w_proj, final_norm):
    batch, seq, d = x.shape
    depth = p.shape[0]
    n = batch * seq
    h = x.transpose(1, 0, 2).reshape(n, d)
    pt = p.transpose(0, 2, 1, 3).reshape(depth, n, PLE_DIM)
    rows = lambda v: v.reshape(depth, 1, -1).astype(F32)
    bf = lambda w: w.astype(BF16)

    lam, bmat, cmat, kmat = jax.vmap(_ssm_params)(ssm_lambda_re, ssm_lambda_im, ssm_log_dt,
                                                  ssm_b_re, ssm_b_im, ssm_c_re, ssm_c_im)
    wpool = jax.vmap(_pool_pairs)(pool_w)
    f1 = (rows(ffn1_norm), bf(ffn1_wi), bf(ffn1_wo))
    f2 = (rows(ffn2_norm), bf(ffn2_wi), bf(ffn2_wo))
    mx = (rows(mix_norm), bf(w_in), bmat, lam, cmat, kmat, rows(ssm_d), bf(ssm_w_glu), wpool,
          rows(pool_scale), bf(w_out))
    gt = (pt, rows(ple_norm), bf(ple_w_gate), bf(ple_w_proj), final_norm.reshape(1, -1).astype(F32))
    for i in range(depth):
        h = _ffn(h, i, *f1)
        h = _mixer(h, i, *mx, batch)
        h = _ffn(h, i, *f2, gate_args=gt, final=(i == depth - 1))
    return h.reshape(seq, batch, d).transpose(1, 0, 2)
```

```python
import functools
import math

import jax
import jax.numpy as jnp
import numpy as np
from jax import lax
from jax.experimental import pallas as pl
from jax.experimental.pallas import tpu as pltpu

D_MODEL = 1024
D_FF = 2816
PLE_DIM = 256
SSM_WIDTH = 512
POOL_WIDTH = 512
SSM_GROUP_CH = 16
SSM_GROUPS = 32
SSM_STATE = 64
POOL_WINDOWS = (2, 4, 8, 16)
POOL_GROUP_CH = 128
EPS = 1e-6

LANES = 128
STATE_LANES = 2 * SSM_GROUPS * SSM_STATE
SLAB = 1024
N_SLABS = STATE_LANES // SLAB
TILE_PAIR = 2 * LANES
POOL_HIST_STEPS = 16

FFN_ROWS = 1024
FFN_CHUNK = 256
MIX_STEPS = 64
VMEM_LIMIT = 56 * 1024 * 1024

F32 = jnp.float32
BF16 = jnp.bfloat16


def _inv_rms(x):
    return lax.rsqrt(jnp.mean(x * x, axis=-1, keepdims=True) + EPS)


def _layer_spec(shape, layer):
    nd = len(shape)
    return pl.BlockSpec((None,) + shape, lambda i: (layer,) + (0,) * nd,
                        pipeline_mode=pl.Buffered(1))


def _const_spec(shape):
    nd = len(shape)
    return pl.BlockSpec(shape, lambda i: (0,) * nd, pipeline_mode=pl.Buffered(1))


def _row_spec(rows, cols):
    return pl.BlockSpec((rows, cols), lambda i: (i, 0))


def _ffn_kernel(*refs, gated, final):
    if gated:
        (x_ref, g_ref, wi_ref, wo_ref, p_ref, pg_ref, wg_ref, wp_ref, fg_ref, o_ref,
         xg_ref, a_ref) = refs
    else:
        (x_ref, g_ref, wi_ref, wo_ref, o_ref, xg_ref, a_ref) = refs
    x = x_ref[...]
    xg_ref[...] = (x * g_ref[...]).astype(BF16)
    r = _inv_rms(x)
    for k in range(D_FF // FFN_CHUNK):
        c0 = k * FFN_CHUNK
        xg = xg_ref[...]
        g = r * jnp.dot(xg, wi_ref[:, c0:c0 + FFN_CHUNK], preferred_element_type=F32)
        u = r * jnp.dot(xg, wi_ref[:, D_FF + c0:D_FF + c0 + FFN_CHUNK], preferred_element_type=F32)
        a_ref[:, c0:c0 + FFN_CHUNK] = (g * jax.nn.sigmoid(g) * u).astype(BF16)
    y = x + 0.5 * jnp.dot(a_ref[...], wo_ref[...], preferred_element_type=F32)
    if gated:
        gate = jax.nn.sigmoid(_inv_rms(y) * jnp.dot(
            (y * pg_ref[...]).astype(BF16), wg_ref[...], preferred_element_type=F32))
        proj = jnp.dot(p_ref[...].astype(BF16), wp_ref[...], preferred_element_type=F32)
        y = y + gate * proj
        if final:
            y = y * _inv_rms(y) * fg_ref[...]
    o_ref[...] = y


def _ffn(h, layer, g, wi, wo, gate_args=None, final=False):
    n = h.shape[0]
    in_specs = [
        _row_spec(FFN_ROWS, D_MODEL),
        _layer_spec((1, D_MODEL), layer),
        _layer_spec((D_MODEL, 2 * D_FF), layer),
        _layer_spec((D_FF, D_MODEL), layer),
    ]
    args = [h, g, wi, wo]
    if gate_args is not None:
        in_specs += [
            pl.BlockSpec((None, FFN_ROWS, PLE_DIM), lambda i: (layer, i, 0)),
            _layer_spec((1, D_MODEL), layer),
            _layer_spec((D_MODEL, D_MODEL), layer),
            _layer_spec((PLE_DIM, D_MODEL), layer),
            _const_spec((1, D_MODEL)),
        ]
        args += list(gate_args)
    return pl.pallas_call(
        functools.partial(_ffn_kernel, gated=gate_args is not None, final=final),
        grid=(n // FFN_ROWS,),
        in_specs=in_specs,
        out_specs=_row_spec(FFN_ROWS, D_MODEL),
        out_shape=jax.ShapeDtypeStruct((n, D_MODEL), F32),
        scratch_shapes=[pltpu.VMEM((FFN_ROWS, D_MODEL), BF16),
                        pltpu.VMEM((FFN_ROWS, D_FF), BF16)],
        compiler_params=pltpu.CompilerParams(
            dimension_semantics=("arbitrary",), vmem_limit_bytes=VMEM_LIMIT),
        name="ffn_gate" if gate_args is not None else "ffn",
    )(*args)


def _mixer_kernel(h_ref, g_ref, win_ref, bmat_ref, lam_ref, cmat_ref, kmat_ref, dskip_ref, wglu_ref,
                  wpool_ref, pscale_ref, wout_ref, o_ref,
                  bu_ref, st_ref, carry_ref, hist_ref, *, batch):
    rows = h_ref.shape[0]
    steps = rows // batch
    hist_rows = POOL_HIST_STEPS * batch
    chunk = pl.program_id(0)

    @pl.when(chunk == 0)
    def _():
        carry_ref[...] = jnp.zeros_like(carry_ref)
        hist_ref[...] = jnp.zeros_like(hist_ref)

    h = h_ref[...]
    z = _inv_rms(h) * jnp.dot((h * g_ref[...]).astype(BF16), win_ref[...],
                              preferred_element_type=F32)
    u = z[:, :SSM_WIDTH]
    zp = z[:, SSM_WIDTH:]

    pairs = steps // 2
    ub = u.astype(BF16)
    ub_even = jnp.concatenate([ub[2 * m * batch:(2 * m + 1) * batch] for m in range(pairs)], axis=0)
    ub_odd = jnp.concatenate([ub[(2 * m + 1) * batch:(2 * m + 2) * batch] for m in range(pairs)],
                             axis=0)
    u2 = [jnp.concatenate([ub_even[:, q * LANES:(q + 1) * LANES],
                           ub_odd[:, q * LANES:(q + 1) * LANES]], axis=1) for q in range(N_SLABS)]
    for q in range(N_SLABS):
        bu_ref[:, q * SLAB:(q + 1) * SLAB] = jnp.dot(u2[q], bmat_ref[q],
                                                    preferred_element_type=F32)

    for q in range(N_SLABS):
        n_tp = SLAB // TILE_PAIR
        a_re = [lam_ref[:, q * SLAB + j * TILE_PAIR:q * SLAB + j * TILE_PAIR + LANES]
                for j in range(n_tp)]
        a_im = [lam_ref[:, q * SLAB + j * TILE_PAIR + LANES:q * SLAB + (j + 1) * TILE_PAIR]
                for j in range(n_tp)]
        init = carry_ref[:, q * SLAB:(q + 1) * SLAB]
        s = [init[:, i * LANES:(i + 1) * LANES] for i in range(2 * n_tp)]
        for m in range(pairs):
            r0 = m * batch
            st_ref[r0:r0 + batch, q * SLAB:(q + 1) * SLAB] = jnp.concatenate(s, axis=1).astype(BF16)
            bu = bu_ref[r0:r0 + batch, q * SLAB:(q + 1) * SLAB]
            new = []
            for j in range(n_tp):
                s_re, s_im = s[2 * j], s[2 * j + 1]
                b_re = bu[:, j * TILE_PAIR:j * TILE_PAIR + LANES]
                b_im = bu[:, j * TILE_PAIR + LANES:(j + 1) * TILE_PAIR]
                new.append(a_re[j] * s_re - a_im[j] * s_im + b_re)
                new.append(a_re[j] * s_im + a_im[j] * s_re + b_im)
            s = new
        carry_ref[:, q * SLAB:(q + 1) * SLAB] = jnp.concatenate(s, axis=1)

    y2 = [jnp.dot(st_ref[:, q * SLAB:(q + 1) * SLAB], cmat_ref[q], preferred_element_type=F32)
          + jnp.dot(u2[q], kmat_ref[q], preferred_element_type=F32) for q in range(N_SLABS)]
    y_even = jnp.concatenate([y2[q][:, :LANES] for q in range(N_SLABS)], axis=1)
    y_odd = jnp.concatenate([y2[q][:, LANES:] for q in range(N_SLABS)], axis=1)
    y = jnp.concatenate(
        [blk[m * batch:(m + 1) * batch] for m in range(pairs) for blk in (y_even, y_odd)], axis=0)

    y = y + dskip_ref[...] * u
    y = jax.nn.gelu(y, approximate=True)
    y = y * jax.nn.sigmoid(jnp.dot(y.astype(BF16), wglu_ref[...], preferred_element_type=F32))

    ext = jnp.concatenate([hist_ref[...], zp], axis=0)
    hist_ref[...] = zp[rows - hist_rows:, :]
    t_idx = chunk * steps + lax.shift_right_logical(
        lax.broadcasted_iota(jnp.int32, (rows, LANES), 0), int(math.log2(batch)))
    pooled = []
    for gi, win in enumerate(POOL_WINDOWS):
        col = ext[:, gi * LANES:(gi + 1) * LANES]
        off = 0
        span = 1
        while span < win:
            sh = span * batch
            col = col[sh:, :] + col[:-sh, :]
            off += sh
            span *= 2
        wsum = col[hist_rows - off:hist_rows - off + rows, :]
        count = jnp.minimum(t_idx + 1, win).astype(F32)
        pooled.append(wsum / count - zp[:, gi * LANES:(gi + 1) * LANES])
    pooled = jnp.concatenate(pooled, axis=1).astype(BF16)
    yp = jnp.concatenate(
        [jnp.dot(pooled[:, k * TILE_PAIR:(k + 1) * TILE_PAIR], wpool_ref[k],
                 preferred_element_type=F32) for k in range(2)], axis=1)
    yp = yp * pscale_ref[...]

    mix = jnp.concatenate([y, yp], axis=1).astype(BF16)
    o_ref[...] = h + jnp.dot(mix, wout_ref[...], preferred_element_type=F32)


def _mixer(h, layer, g, win, bmat, lam, cmat, kmat, dskip, wglu, wpool, pscale, wout, batch):
    n = h.shape[0]
    rows = MIX_STEPS * batch
    return pl.pallas_call(
        functools.partial(_mixer_kernel, batch=batch),
        grid=(n // rows,),
        in_specs=[
            _row_spec(rows, D_MODEL),
            _layer_spec((1, D_MODEL), layer),
            _layer_spec((D_MODEL, D_MODEL), layer),
            _layer_spec((N_SLABS, TILE_PAIR, SLAB), layer),
            _layer_spec((1, STATE_LANES), layer),
            _layer_spec((N_SLABS, SLAB, TILE_PAIR), layer),
            _layer_spec((N_SLABS, TILE_PAIR, TILE_PAIR), layer),
            _layer_spec((1, SSM_WIDTH), layer),
            _layer_spec((SSM_WIDTH, SSM_WIDTH), layer),
            _layer_spec((2, TILE_PAIR, TILE_PAIR), layer),
            _layer_spec((1, POOL_WIDTH), layer),
            _layer_spec((D_MODEL, D_MODEL), layer),
        ],
        out_specs=_row_spec(rows, D_MODEL),
        out_shape=jax.ShapeDtypeStruct((n, D_MODEL), F32),
        scratch_shapes=[
            pltpu.VMEM((rows // 2, STATE_LANES), F32),
            pltpu.VMEM((rows // 2, STATE_LANES), BF16),
            pltpu.VMEM((batch, STATE_LANES), F32),
            pltpu.VMEM((POOL_HIST_STEPS * batch, POOL_WIDTH), F32),
        ],
        compiler_params=pltpu.CompilerParams(
            dimension_semantics=("arbitrary",), vmem_limit_bytes=VMEM_LIMIT),
        name="mixer",
    )(h, g, win, bmat, lam, cmat, kmat, dskip, wglu, wpool, pscale, wout)


def _ssm_params(lam_re, lam_im, log_dt, b_re, b_im, c_re, c_im):
    dt = jnp.exp(log_dt)[:, None]
    mag = jnp.exp(lam_re * dt)
    ab_re = mag * jnp.cos(lam_im * dt)
    ab_im = mag * jnp.sin(lam_im * dt)
    den = lam_re * lam_re + lam_im * lam_im
    f_re = ((ab_re - 1.0) * lam_re + ab_im * lam_im) / den
    f_im = (ab_im * lam_re - (ab_re - 1.0) * lam_im) / den
    bb_re = f_re[..., None] * b_re - f_im[..., None] * b_im
    bb_im = f_re[..., None] * b_im + f_im[..., None] * b_re

    n_q = N_SLABS
    gl = SSM_GROUPS // n_q
    ch = gl * SSM_GROUP_CH
    lane = np.arange(SLAB)
    lane_ri_p = (lane // LANES % 2) * SSM_STATE + lane % SSM_STATE
    lane_gl = 2 * (lane // TILE_PAIR) + lane // SSM_STATE % 2
    spread = jnp.asarray(np.arange(2 * SSM_STATE)[:, None] == lane_ri_p[None, :], BF16)
    mask = jnp.asarray(np.arange(ch)[:, None] // SSM_GROUP_CH == lane_gl[None, :], F32)
    spread_d = jnp.asarray(np.arange(SSM_GROUP_CH)[:, None] == np.arange(ch)[None, :] % SSM_GROUP_CH,
                           BF16)
    mask_d = jnp.asarray(np.arange(ch)[:, None] // SSM_GROUP_CH
                         == np.arange(ch)[None, :] // SSM_GROUP_CH, F32)

    def embed_in(re, im):
        v = jnp.stack([re, im], axis=0).reshape(2, n_q, gl, SSM_STATE, SSM_GROUP_CH)
        v = v.transpose(1, 2, 4, 0, 3).reshape(n_q, ch, 2 * SSM_STATE)
        return jnp.einsum('qck,kl->qcl', v.astype(BF16), spread,
                          preferred_element_type=F32) * mask

    def embed_out(re, im):
        v = jnp.stack([re, -im], axis=0).reshape(2, n_q, gl, SSM_GROUP_CH, SSM_STATE)
        v = v.transpose(1, 0, 4, 2, 3).reshape(n_q, 2 * SSM_STATE, ch)
        return jnp.einsum('kl,qkc->qlc', spread, v.astype(BF16),
                          preferred_element_type=F32) * mask.T

    def embed_direct(k):
        v = k.reshape(n_q, ch, SSM_GROUP_CH)
        return jnp.einsum('qck,kl->qcl', v.astype(BF16), spread_d,
                          preferred_element_type=F32) * mask_d

    a2_re = ab_re * ab_re - ab_im * ab_im
    a2_im = 2.0 * ab_re * ab_im
    lb_re = ab_re[..., None] * bb_re - ab_im[..., None] * bb_im
    lb_im = ab_re[..., None] * bb_im + ab_im[..., None] * bb_re
    cl1_re = c_re * ab_re[:, None, :] - c_im * ab_im[:, None, :]
    cl1_im = c_re * ab_im[:, None, :] + c_im * ab_re[:, None, :]
    cl2_re = c_re * a2_re[:, None, :] - c_im * a2_im[:, None, :]
    cl2_im = c_re * a2_im[:, None, :] + c_im * a2_re[:, None, :]

    lam_lanes = jnp.stack([a2_re.reshape(SSM_GROUPS // 2, 2, SSM_STATE),
                           a2_im.reshape(SSM_GROUPS // 2, 2, SSM_STATE)], axis=1)
    lam_lanes = lam_lanes.reshape(1, STATE_LANES)

    bmat = jnp.concatenate([embed_in(lb_re, lb_im), embed_in(bb_re, bb_im)], axis=1)
    cmat = jnp.concatenate([embed_out(cl1_re, cl1_im), embed_out(cl2_re, cl2_im)], axis=2)
    ein = functools.partial(jnp.einsum, 'gop,gpi->gio', precision=lax.Precision.HIGHEST)
    k0 = embed_direct(ein(c_re, bb_re) - ein(c_im, bb_im))
    k1 = embed_direct(ein(c_re, lb_re) - ein(c_im, lb_im))
    kmat = jnp.concatenate([jnp.concatenate([k0, k1], axis=2),
                            jnp.concatenate([jnp.zeros_like(k0), k0], axis=2)], axis=1)
    return lam_lanes, bmat.astype(BF16), cmat.astype(BF16), kmat.astype(BF16)


def _pool_pairs(w_pool):
    zeros = jnp.zeros((POOL_GROUP_CH, POOL_GROUP_CH), w_pool.dtype)
    pairs = [jnp.block([[w_pool[2 * k], zeros], [zeros, w_pool[2 * k + 1]]]) for k in range(2)]
    return jnp.stack(pairs, axis=0).astype(BF16)


def kernel(x, p, ffn1_norm, ffn1_wi, ffn1_wo, mix_norm, w_in, ssm_lambda_re, ssm_lambda_im, ssm_log_dt, ssm_b_re, ssm_b_im, ssm_c_re, ssm_c_im, ssm_d, ssm_w_glu, pool_w, pool_scale, w_out, ffn2_norm, ffn2_wi, ffn2_wo, ple_norm, ple_w_gate, ple_w_proj, final_norm):
    batch, seq, d = x.shape
    depth = p.shape[0]
    n = batch * seq
    h = x.transpose(1, 0, 2).reshape(n, d)
    pt = p.transpose(0, 2, 1, 3).reshape(depth, n, PLE_DIM)
    rows = lambda v: v.reshape(depth, 1, -1).astype(F32)
    bf = lambda w: w.astype(BF16)

    lam, bmat, cmat, kmat = jax.vmap(_ssm_params)(ssm_lambda_re, ssm_lambda_im, ssm_log_dt,
                                                  ssm_b_re, ssm_b_im, ssm_c_re, ssm_c_im)
    wpool = jax.vmap(_pool_pairs)(pool_w)
    f1 = (rows(ffn1_norm), bf(ffn1_wi), bf(ffn1_wo))
    f2 = (rows(ffn2_norm), bf(ffn2_wi), bf(ffn2_wo))
    mx = (rows(mix_norm), bf(w_in), bmat, lam, cmat, kmat, rows(ssm_d), bf(ssm_w_glu), wpool,
          rows(pool_scale), bf(w_out))
    gt = (pt, rows(ple_norm), bf(ple_w_gate), bf(ple_w_proj), final_norm.reshape(1, -1).astype(F32))
    for i in range(depth):
        h = _ffn(h, i, *f1)
        h = _mixer(h, i, *mx, batch)
        h = _ffn(h, i, *f2, gate_args=gt, final=(i == depth - 1))
    return h.reshape(seq, batch, d).transpose(1, 0, 2)
```

```python
import functools
import math

import jax
import jax.numpy as jnp
import numpy as np
from jax import lax
from jax.experimental import pallas as pl
from jax.experimental.pallas import tpu as pltpu

D_MODEL = 1024
D_FF = 2816
PLE_DIM = 256
SSM_WIDTH = 512
POOL_WIDTH = 512
SSM_GROUP_CH = 16
SSM_GROUPS = 32
SSM_STATE = 64
POOL_WINDOWS = (2, 4, 8, 16)
POOL_GROUP_CH = 128
EPS = 1e-6

LANES = 128
SUBLANES = 8
STATE_LANES = 2 * SSM_GROUPS * SSM_STATE
SLAB = 1024
N_SLABS = STATE_LANES // SLAB
TILE_PAIR = 2 * LANES
POOL_HIST_STEPS = 16

FFN_ROWS = 1024
FFN_CHUNK = 256
MIX_STEPS = 64
VMEM_LIMIT = 56 * 1024 * 1024

F32 = jnp.float32
BF16 = jnp.bfloat16


def _inv_rms(x):
    return lax.rsqrt(jnp.mean(x * x, axis=-1, keepdims=True) + EPS)


def _layer_spec(shape, layer):
    nd = len(shape)
    return pl.BlockSpec((None,) + shape, lambda i: (layer,) + (0,) * nd,
                        pipeline_mode=pl.Buffered(1))


def _const_spec(shape):
    nd = len(shape)
    return pl.BlockSpec(shape, lambda i: (0,) * nd, pipeline_mode=pl.Buffered(1))


def _row_spec(rows, cols):
    return pl.BlockSpec((rows, cols), lambda i: (i, 0))


def _slab_pitch(steps):
    return steps + SUBLANES


def _to_time_major(src_ref, slab_ref):
    batch, steps, cols = src_ref.shape
    n_slab = cols // LANES
    pitch = _slab_pitch(steps)
    for b in range(batch):
        for c in range(n_slab):
            slab_ref[c, b * pitch:b * pitch + steps, :] = src_ref[b, :, c * LANES:(c + 1) * LANES]
    return jnp.concatenate(
        [jnp.concatenate([slab_ref[c, pl.ds(t, batch, stride=pitch), :] for c in range(n_slab)],
                         axis=1) for t in range(steps)], axis=0)


def _from_time_major(y, slab_ref, dst_ref):
    batch, steps, cols = dst_ref.shape
    n_slab = cols // LANES
    pitch = _slab_pitch(steps)
    for t in range(steps):
        for c in range(n_slab):
            slab_ref[c, pl.ds(t, batch, stride=pitch), :] = (
                y[t * batch:(t + 1) * batch, c * LANES:(c + 1) * LANES])
    for b in range(batch):
        for c in range(n_slab):
            dst_ref[b, :, c * LANES:(c + 1) * LANES] = slab_ref[c, b * pitch:b * pitch + steps, :]


def _ffn_kernel(*refs, gated, final, x_natural):
    refs = list(refs)
    slab_ref = refs.pop() if (gated or x_natural) else None
    a_ref = refs.pop()
    xg_ref = refs.pop()
    o_ref = refs.pop()
    x_ref, g_ref, wi_ref, wo_ref = refs[:4]
    x = _to_time_major(x_ref, slab_ref) if x_natural else x_ref[...]
    xg_ref[...] = (x * g_ref[...]).astype(BF16)
    r = _inv_rms(x)
    for k in range(D_FF // FFN_CHUNK):
        c0 = k * FFN_CHUNK
        xg = xg_ref[...]
        g = r * jnp.dot(xg, wi_ref[:, c0:c0 + FFN_CHUNK], preferred_element_type=F32)
        u = r * jnp.dot(xg, wi_ref[:, D_FF + c0:D_FF + c0 + FFN_CHUNK], preferred_element_type=F32)
        a_ref[:, c0:c0 + FFN_CHUNK] = (g * jax.nn.sigmoid(g) * u).astype(BF16)
    y = x + 0.5 * jnp.dot(a_ref[...], wo_ref[...], preferred_element_type=F32)
    if gated:
        p_ref, pg_ref, wg_ref, wp_ref, fg_ref = refs[4:]
        pe = _to_time_major(p_ref, slab_ref).astype(BF16)
        gate = jax.nn.sigmoid(_inv_rms(y) * jnp.dot(
            (y * pg_ref[...]).astype(BF16), wg_ref[...], preferred_element_type=F32))
        y = y + gate * jnp.dot(pe, wp_ref[...], preferred_element_type=F32)
    if final:
        y = y * _inv_rms(y) * fg_ref[...]
        _from_time_major(y, slab_ref, o_ref)
    else:
        o_ref[...] = y


def _ffn(h, layer, batch, g, wi, wo, gate_args=None, final=False, x_natural=False):
    steps = FFN_ROWS // batch
    natural_spec = pl.BlockSpec((batch, steps, D_MODEL), lambda i: (0, i, 0))
    if x_natural:
        n = h.shape[0] * h.shape[1]
        x_spec = natural_spec
    else:
        n = h.shape[0]
        x_spec = _row_spec(FFN_ROWS, D_MODEL)
    in_specs = [
        x_spec,
        _layer_spec((1, D_MODEL), layer),
        _layer_spec((D_MODEL, 2 * D_FF), layer),
        _layer_spec((D_FF, D_MODEL), layer),
    ]
    args = [h, g, wi, wo]
    scratch = [pltpu.VMEM((FFN_ROWS, D_MODEL), BF16),
               pltpu.VMEM((FFN_ROWS, D_FF), BF16)]
    gated = gate_args is not None
    assert gated or not final
    if gated:
        in_specs += [
            pl.BlockSpec((None, batch, steps, PLE_DIM), lambda i: (layer, 0, i, 0)),
            _layer_spec((1, D_MODEL), layer),
            _layer_spec((D_MODEL, D_MODEL), layer),
            _layer_spec((PLE_DIM, D_MODEL), layer),
            _const_spec((1, D_MODEL)),
        ]
        args += list(gate_args)
    if gated or x_natural:
        n_slab = (D_MODEL if (final or x_natural) else PLE_DIM) // LANES
        scratch.append(pltpu.VMEM((n_slab, batch * _slab_pitch(steps), LANES), F32))
    if final:
        out_specs = natural_spec
        out_shape = jax.ShapeDtypeStruct((batch, n // batch, D_MODEL), F32)
    else:
        out_specs = _row_spec(FFN_ROWS, D_MODEL)
        out_shape = jax.ShapeDtypeStruct((n, D_MODEL), F32)
    return pl.pallas_call(
        functools.partial(_ffn_kernel, gated=gated, final=final, x_natural=x_natural),
        grid=(n // FFN_ROWS,),
        in_specs=in_specs,
        out_specs=out_specs,
        out_shape=out_shape,
        scratch_shapes=scratch,
        compiler_params=pltpu.CompilerParams(
            dimension_semantics=("arbitrary",), vmem_limit_bytes=VMEM_LIMIT),
        name="ffn_gate" if gated else "ffn",
    )(*args)


def _mixer_kernel(h_ref, g_ref, win_ref, bmat_ref, lam_ref, cmat_ref, kmat_ref, dskip_ref, wglu_ref,
                  wpool_ref, pscale_ref, wout_ref, o_ref,
                  bu_ref, st_ref, carry_ref, hist_ref, *, batch):
    rows = h_ref.shape[0]
    steps = rows // batch
    hist_rows = POOL_HIST_STEPS * batch
    chunk = pl.program_id(0)

    @pl.when(chunk == 0)
    def _():
        carry_ref[...] = jnp.zeros_like(carry_ref)
        hist_ref[...] = jnp.zeros_like(hist_ref)

    h = h_ref[...]
    z = _inv_rms(h) * jnp.dot((h * g_ref[...]).astype(BF16), win_ref[...],
                              preferred_element_type=F32)
    u = z[:, :SSM_WIDTH]
    zp = z[:, SSM_WIDTH:]

    pairs = steps // 2
    ub = u.astype(BF16)
    ub_even = jnp.concatenate([ub[2 * m * batch:(2 * m + 1) * batch] for m in range(pairs)], axis=0)
    ub_odd = jnp.concatenate([ub[(2 * m + 1) * batch:(2 * m + 2) * batch] for m in range(pairs)],
                             axis=0)
    u2 = [jnp.concatenate([ub_even[:, q * LANES:(q + 1) * LANES],
                           ub_odd[:, q * LANES:(q + 1) * LANES]], axis=1) for q in range(N_SLABS)]
    for q in range(N_SLABS):
        bu_ref[:, q * SLAB:(q + 1) * SLAB] = jnp.dot(u2[q], bmat_ref[q],
                                                    preferred_element_type=F32)

    for q in range(N_SLABS):
        n_tp = SLAB // TILE_PAIR
        a_re = [lam_ref[:, q * SLAB + j * TILE_PAIR:q * SLAB + j * TILE_PAIR + LANES]
                for j in range(n_tp)]
        a_im = [lam_ref[:, q * SLAB + j * TILE_PAIR + LANES:q * SLAB + (j + 1) * TILE_PAIR]
                for j in range(n_tp)]
        init = carry_ref[:, q * SLAB:(q + 1) * SLAB]
        s = [init[:, i * LANES:(i + 1) * LANES] for i in range(2 * n_tp)]
        for m in range(pairs):
            r0 = m * batch
            st_ref[r0:r0 + batch, q * SLAB:(q + 1) * SLAB] = jnp.concatenate(s, axis=1).astype(BF16)
            bu = bu_ref[r0:r0 + batch, q * SLAB:(q + 1) * SLAB]
            new = []
            for j in range(n_tp):
                s_re, s_im = s[2 * j], s[2 * j + 1]
                b_re = bu[:, j * TILE_PAIR:j * TILE_PAIR + LANES]
                b_im = bu[:, j * TILE_PAIR + LANES:(j + 1) * TILE_PAIR]
                new.append(a_re[j] * s_re - a_im[j] * s_im + b_re)
                new.append(a_re[j] * s_im + a_im[j] * s_re + b_im)
            s = new
        carry_ref[:, q * SLAB:(q + 1) * SLAB] = jnp.concatenate(s, axis=1)

    y2 = [jnp.dot(st_ref[:, q * SLAB:(q + 1) * SLAB], cmat_ref[q], preferred_element_type=F32)
          + jnp.dot(u2[q], kmat_ref[q], preferred_element_type=F32) for q in range(N_SLABS)]
    y_even = jnp.concatenate([y2[q][:, :LANES] for q in range(N_SLABS)], axis=1)
    y_odd = jnp.concatenate([y2[q][:, LANES:] for q in range(N_SLABS)], axis=1)
    y = jnp.concatenate(
        [blk[m * batch:(m + 1) * batch] for m in range(pairs) for blk in (y_even, y_odd)], axis=0)

    y = y + dskip_ref[...] * u
    y = jax.nn.gelu(y, approximate=True)
    y = y * jax.nn.sigmoid(jnp.dot(y.astype(BF16), wglu_ref[...], preferred_element_type=F32))

    ext = jnp.concatenate([hist_ref[...], zp], axis=0)
    hist_ref[...] = zp[rows - hist_rows:, :]
    t_idx = chunk * steps + lax.shift_right_logical(
        lax.broadcasted_iota(jnp.int32, (rows, LANES), 0), int(math.log2(batch)))
    pooled = []
    for gi, win in enumerate(POOL_WINDOWS):
        col = ext[:, gi * LANES:(gi + 1) * LANES]
        off = 0
        span = 1
        while span < win:
            sh = span * batch
            col = col[sh:, :] + col[:-sh, :]
            off += sh
            span *= 2
        wsum = col[hist_rows - off:hist_rows - off + rows, :]
        count = jnp.minimum(t_idx + 1, win).astype(F32)
        pooled.append(wsum / count - zp[:, gi * LANES:(gi + 1) * LANES])
    pooled = jnp.concatenate(pooled, axis=1).astype(BF16)
    yp = jnp.concatenate(
        [jnp.dot(pooled[:, k * TILE_PAIR:(k + 1) * TILE_PAIR], wpool_ref[k],
                 preferred_element_type=F32) for k in range(2)], axis=1)
    yp = yp * pscale_ref[...]

    mix = jnp.concatenate([y, yp], axis=1).astype(BF16)
    o_ref[...] = h + jnp.dot(mix, wout_ref[...], preferred_element_type=F32)


def _mixer(h, layer, g, win, bmat, lam, cmat, kmat, dskip, wglu, wpool, pscale, wout, batch):
    n = h.shape[0]
    rows = MIX_STEPS * batch
    return pl.pallas_call(
        functools.partial(_mixer_kernel, batch=batch),
        grid=(n // rows,),
        in_specs=[
            _row_spec(rows, D_MODEL),
            _layer_spec((1, D_MODEL), layer),
            _layer_spec((D_MODEL, D_MODEL), layer),
            _layer_spec((N_SLABS, TILE_PAIR, SLAB), layer),
            _layer_spec((1, STATE_LANES), layer),
            _layer_spec((N_SLABS, SLAB, TILE_PAIR), layer),
            _layer_spec((N_SLABS, TILE_PAIR, TILE_PAIR), layer),
            _layer_spec((1, SSM_WIDTH), layer),
            _layer_spec((SSM_WIDTH, SSM_WIDTH), layer),
            _layer_spec((2, TILE_PAIR, TILE_PAIR), layer),
            _layer_spec((1, POOL_WIDTH), layer),
            _layer_spec((D_MODEL, D_MODEL), layer),
        ],
        out_specs=_row_spec(rows, D_MODEL),
        out_shape=jax.ShapeDtypeStruct((n, D_MODEL), F32),
        scratch_shapes=[
            pltpu.VMEM((rows // 2, STATE_LANES), F32),
            pltpu.VMEM((rows // 2, STATE_LANES), BF16),
            pltpu.VMEM((batch, STATE_LANES), F32),
            pltpu.VMEM((POOL_HIST_STEPS * batch, POOL_WIDTH), F32),
        ],
        compiler_params=pltpu.CompilerParams(
            dimension_semantics=("arbitrary",), vmem_limit_bytes=VMEM_LIMIT),
        name="mixer",
    )(h, g, win, bmat, lam, cmat, kmat, dskip, wglu, wpool, pscale, wout)


def _ssm_params(lam_re, lam_im, log_dt, b_re, b_im, c_re, c_im):
    dt = jnp.exp(log_dt)[:, None]
    mag = jnp.exp(lam_re * dt)
    ab_re = mag * jnp.cos(lam_im * dt)
    ab_im = mag * jnp.sin(lam_im * dt)
    den = lam_re * lam_re + lam_im * lam_im
    f_re = ((ab_re - 1.0) * lam_re + ab_im * lam_im) / den
    f_im = (ab_im * lam_re - (ab_re - 1.0) * lam_im) / den
    bb_re = f_re[..., None] * b_re - f_im[..., None] * b_im
    bb_im = f_re[..., None] * b_im + f_im[..., None] * b_re

    n_q = N_SLABS
    gl = SSM_GROUPS // n_q
    ch = gl * SSM_GROUP_CH
    lane = np.arange(SLAB)
    lane_ri_p = (lane // LANES % 2) * SSM_STATE + lane % SSM_STATE
    lane_gl = 2 * (lane // TILE_PAIR) + lane // SSM_STATE % 2
    spread = jnp.asarray(np.arange(2 * SSM_STATE)[:, None] == lane_ri_p[None, :], BF16)
    mask = jnp.asarray(np.arange(ch)[:, None] // SSM_GROUP_CH == lane_gl[None, :], F32)
    spread_d = jnp.asarray(np.arange(SSM_GROUP_CH)[:, None] == np.arange(ch)[None, :] % SSM_GROUP_CH,
                           BF16)
    mask_d = jnp.asarray(np.arange(ch)[:, None] // SSM_GROUP_CH
                         == np.arange(ch)[None, :] // SSM_GROUP_CH, F32)

    def embed_in(re, im):
        v = jnp.stack([re, im], axis=0).reshape(2, n_q, gl, SSM_STATE, SSM_GROUP_CH)
        v = v.transpose(1, 2, 4, 0, 3).reshape(n_q, ch, 2 * SSM_STATE)
        return jnp.einsum('qck,kl->qcl', v.astype(BF16), spread,
                          preferred_element_type=F32) * mask

    def embed_out(re, im):
        v = jnp.stack([re, -im], axis=0).reshape(2, n_q, gl, SSM_GROUP_CH, SSM_STATE)
        v = v.transpose(1, 0, 4, 2, 3).reshape(n_q, 2 * SSM_STATE, ch)
        return jnp.einsum('kl,qkc->qlc', spread, v.astype(BF16),
                          preferred_element_type=F32) * mask.T

    def embed_direct(k):
        v = k.reshape(n_q, ch, SSM_GROUP_CH)
        return jnp.einsum('qck,kl->qcl', v.astype(BF16), spread_d,
                          preferred_element_type=F32) * mask_d

    a2_re = ab_re * ab_re - ab_im * ab_im
    a2_im = 2.0 * ab_re * ab_im
    lb_re = ab_re[..., None] * bb_re - ab_im[..., None] * bb_im
    lb_im = ab_re[..., None] * bb_im + ab_im[..., None] * bb_re
    cl1_re = c_re * ab_re[:, None, :] - c_im * ab_im[:, None, :]
    cl1_im = c_re * ab_im[:, None, :] + c_im * ab_re[:, None, :]
    cl2_re = c_re * a2_re[:, None, :] - c_im * a2_im[:, None, :]
    cl2_im = c_re * a2_im[:, None, :] + c_im * a2_re[:, None, :]

    lam_lanes = jnp.stack([a2_re.reshape(SSM_GROUPS // 2, 2, SSM_STATE),
                           a2_im.reshape(SSM_GROUPS // 2, 2, SSM_STATE)], axis=1)
    lam_lanes = lam_lanes.reshape(1, STATE_LANES)

    bmat = jnp.concatenate([embed_in(lb_re, lb_im), embed_in(bb_re, bb_im)], axis=1)
    cmat = jnp.concatenate([embed_out(cl1_re, cl1_im), embed_out(cl2_re, cl2_im)], axis=2)
    ein = functools.partial(jnp.einsum, 'gop,gpi->gio', precision=lax.Precision.HIGHEST)
    k0 = embed_direct(ein(c_re, bb_re) - ein(c_im, bb_im))
    k1 = embed_direct(ein(c_re, lb_re) - ein(c_im, lb_im))
    kmat = jnp.concatenate([jnp.concatenate([k0, k1], axis=2),
                            jnp.concatenate([jnp.zeros_like(k0), k0], axis=2)], axis=1)
    return lam_lanes, bmat.astype(BF16), cmat.astype(BF16), kmat.astype(BF16)


def _pool_pairs(w_pool):
    zeros = jnp.zeros((POOL_GROUP_CH, POOL_GROUP_CH), w_pool.dtype)
    pairs = [jnp.block([[w_pool[2 * k], zeros], [zeros, w_pool[2 * k + 1]]]) for k in range(2)]
    return jnp.stack(pairs, axis=0).astype(BF16)


def kernel(x, p, ffn1_norm, ffn1_wi, ffn1_wo, mix_norm, w_in, ssm_lambda_re, ssm_lambda_im, ssm_log_dt, ssm_b_re, ssm_b_im, ssm_c_re, ssm_c_im, ssm_d, ssm_w_glu, pool_w, pool_scale, w_out, ffn2_norm, ffn2_wi, ffn2_wo, ple_norm, ple_w_gate, ple_w_proj, final_norm):
    batch, seq, d = x.shape
    depth = p.shape[0]
    rows = lambda v: v.reshape(depth, 1, -1).astype(F32)
    bf = lambda w: w.astype(BF16)

    lam, bmat, cmat, kmat = jax.vmap(_ssm_params)(ssm_lambda_re, ssm_lambda_im, ssm_log_dt,
                                                  ssm_b_re, ssm_b_im, ssm_c_re, ssm_c_im)
    wpool = jax.vmap(_pool_pairs)(pool_w)
    f1 = (rows(ffn1_norm), bf(ffn1_wi), bf(ffn1_wo))
    f2 = (rows(ffn2_norm), bf(ffn2_wi), bf(ffn2_wo))
    mx = (rows(mix_norm), bf(w_in), bmat, lam, cmat, kmat, rows(ssm_d), bf(ssm_w_glu), wpool,
          rows(pool_scale), bf(w_out))
    gt = (p, rows(ple_norm), bf(ple_w_gate), bf(ple_w_proj), final_norm.reshape(1, -1).astype(F32))
    h = x
    for i in range(depth):
        h = _ffn(h, i, batch, *f1, x_natural=(i == 0))
        h = _mixer(h, i, *mx, batch)
        h = _ffn(h, i, batch, *f2, gate_args=gt, final=(i == depth - 1))
    return h
```

```python
import functools
import math

import jax
import jax.numpy as jnp
import numpy as np
from jax import lax
from jax.experimental import pallas as pl
from jax.experimental.pallas import tpu as pltpu

D_MODEL = 1024
D_FF = 2816
PLE_DIM = 256
SSM_WIDTH = 512
POOL_WIDTH = 512
SSM_GROUP_CH = 16
SSM_GROUPS = 32
SSM_STATE = 64
POOL_WINDOWS = (2, 4, 8, 16)
POOL_GROUP_CH = 128
EPS = 1e-6

LANES = 128
SUBLANES = 8
STATE_LANES = 2 * SSM_GROUPS * SSM_STATE
SLAB = 1024
N_SLABS = STATE_LANES // SLAB
TILE_PAIR = 2 * LANES
SSM_BLOCK = 4
UNIT_CH = 2 * LANES // SSM_BLOCK
N_UNITS = SSM_WIDTH // UNIT_CH
UNIT_LANES = STATE_LANES // N_UNITS
POOL_HIST_STEPS = 16

FFN_ROWS = 1024
FFN_CHUNK = 256
MIX_STEPS = 64
VMEM_LIMIT = 56 * 1024 * 1024

F32 = jnp.float32
BF16 = jnp.bfloat16


def _inv_rms(x):
    return lax.rsqrt(jnp.mean(x * x, axis=-1, keepdims=True) + EPS)


def _layer_spec(shape, layer):
    nd = len(shape)
    return pl.BlockSpec((None,) + shape, lambda i: (layer,) + (0,) * nd,
                        pipeline_mode=pl.Buffered(1))


def _const_spec(shape):
    nd = len(shape)
    return pl.BlockSpec(shape, lambda i: (0,) * nd, pipeline_mode=pl.Buffered(1))


def _row_spec(rows, cols):
    return pl.BlockSpec((rows, cols), lambda i: (i, 0))


def _slab_pitch(steps):
    return steps + SUBLANES


def _to_time_major(src_ref, slab_ref):
    batch, steps, cols = src_ref.shape
    n_slab = cols // LANES
    pitch = _slab_pitch(steps)
    for b in range(batch):
        for c in range(n_slab):
            slab_ref[c, b * pitch:b * pitch + steps, :] = src_ref[b, :, c * LANES:(c + 1) * LANES]
    return jnp.concatenate(
        [jnp.concatenate([slab_ref[c, pl.ds(t, batch, stride=pitch), :] for c in range(n_slab)],
                         axis=1) for t in range(steps)], axis=0)


def _from_time_major(y, slab_ref, dst_ref):
    batch, steps, cols = dst_ref.shape
    n_slab = cols // LANES
    pitch = _slab_pitch(steps)
    for t in range(steps):
        for c in range(n_slab):
            slab_ref[c, pl.ds(t, batch, stride=pitch), :] = (
                y[t * batch:(t + 1) * batch, c * LANES:(c + 1) * LANES])
    for b in range(batch):
        for c in range(n_slab):
            dst_ref[b, :, c * LANES:(c + 1) * LANES] = slab_ref[c, b * pitch:b * pitch + steps, :]


def _ffn_kernel(*refs, gated, final, x_natural):
    refs = list(refs)
    slab_ref = refs.pop() if (gated or x_natural) else None
    a_ref = refs.pop()
    xg_ref = refs.pop()
    o_ref = refs.pop()
    x_ref, g_ref, wi_ref, wo_ref = refs[:4]
    x = _to_time_major(x_ref, slab_ref) if x_natural else x_ref[...]
    xg_ref[...] = (x * g_ref[...]).astype(BF16)
    r = _inv_rms(x)
    for k in range(D_FF // FFN_CHUNK):
        c0 = k * FFN_CHUNK
        xg = xg_ref[...]
        g = r * jnp.dot(xg, wi_ref[:, c0:c0 + FFN_CHUNK], preferred_element_type=F32)
        u = r * jnp.dot(xg, wi_ref[:, D_FF + c0:D_FF + c0 + FFN_CHUNK], preferred_element_type=F32)
        a_ref[:, c0:c0 + FFN_CHUNK] = (g * jax.nn.sigmoid(g) * u).astype(BF16)
    y = x + 0.5 * jnp.dot(a_ref[...], wo_ref[...], preferred_element_type=F32)
    if gated:
        p_ref, pg_ref, wg_ref, wp_ref, fg_ref = refs[4:]
        pe = _to_time_major(p_ref, slab_ref).astype(BF16)
        gate = jax.nn.sigmoid(_inv_rms(y) * jnp.dot(
            (y * pg_ref[...]).astype(BF16), wg_ref[...], preferred_element_type=F32))
        y = y + gate * jnp.dot(pe, wp_ref[...], preferred_element_type=F32)
    if final:
        y = y * _inv_rms(y) * fg_ref[...]
        _from_time_major(y, slab_ref, o_ref)
    else:
        o_ref[...] = y


def _ffn(h, layer, batch, g, wi, wo, gate_args=None, final=False, x_natural=False):
    steps = FFN_ROWS // batch
    natural_spec = pl.BlockSpec((batch, steps, D_MODEL), lambda i: (0, i, 0))
    if x_natural:
        n = h.shape[0] * h.shape[1]
        x_spec = natural_spec
    else:
        n = h.shape[0]
        x_spec = _row_spec(FFN_ROWS, D_MODEL)
    in_specs = [
        x_spec,
        _layer_spec((1, D_MODEL), layer),
        _layer_spec((D_MODEL, 2 * D_FF), layer),
        _layer_spec((D_FF, D_MODEL), layer),
    ]
    args = [h, g, wi, wo]
    scratch = [pltpu.VMEM((FFN_ROWS, D_MODEL), BF16),
               pltpu.VMEM((FFN_ROWS, D_FF), BF16)]
    gated = gate_args is not None
    assert gated or not final
    if gated:
        in_specs += [
            pl.BlockSpec((None, batch, steps, PLE_DIM), lambda i: (layer, 0, i, 0)),
            _layer_spec((1, D_MODEL), layer),
            _layer_spec((D_MODEL, D_MODEL), layer),
            _layer_spec((PLE_DIM, D_MODEL), layer),
            _const_spec((1, D_MODEL)),
        ]
        args += list(gate_args)
    if gated or x_natural:
        n_slab = (D_MODEL if (final or x_natural) else PLE_DIM) // LANES
        scratch.append(pltpu.VMEM((n_slab, batch * _slab_pitch(steps), LANES), F32))
    if final:
        out_specs = natural_spec
        out_shape = jax.ShapeDtypeStruct((batch, n // batch, D_MODEL), F32)
    else:
        out_specs = _row_spec(FFN_ROWS, D_MODEL)
        out_shape = jax.ShapeDtypeStruct((n, D_MODEL), F32)
    return pl.pallas_call(
        functools.partial(_ffn_kernel, gated=gated, final=final, x_natural=x_natural),
        grid=(n // FFN_ROWS,),
        in_specs=in_specs,
        out_specs=out_specs,
        out_shape=out_shape,
        scratch_shapes=scratch,
        compiler_params=pltpu.CompilerParams(
            dimension_semantics=("arbitrary",), vmem_limit_bytes=VMEM_LIMIT),
        name="ffn_gate" if gated else "ffn",
    )(*args)


def _mixer_kernel(h_ref, g_ref, win_ref, bmat_ref, lam_ref, cmat_ref, kmat_ref, dskip_ref, wglu_ref,
                  wpool_ref, pscale_ref, wout_ref, o_ref,
                  bu_ref, st_ref, carry_ref, hist_ref, *, batch):
    rows = h_ref.shape[0]
    steps = rows // batch
    hist_rows = POOL_HIST_STEPS * batch
    chunk = pl.program_id(0)

    @pl.when(chunk == 0)
    def _():
        carry_ref[...] = jnp.zeros_like(carry_ref)
        hist_ref[...] = jnp.zeros_like(hist_ref)

    h = h_ref[...]
    z = _inv_rms(h) * jnp.dot((h * g_ref[...]).astype(BF16), win_ref[...],
                              preferred_element_type=F32)
    u = z[:, :SSM_WIDTH]
    zp = z[:, SSM_WIDTH:]

    blocks = steps // SSM_BLOCK
    brows = blocks * batch
    u_step = [jnp.concatenate([u[(SSM_BLOCK * m + i) * batch:(SSM_BLOCK * m + i + 1) * batch]
                               for m in range(blocks)], axis=0) for i in range(SSM_BLOCK)]
    low_half = lax.broadcasted_iota(jnp.int32, (brows, LANES), 1) < UNIT_CH

    def halves(first, second, keep_low):
        a = first if keep_low else pltpu.roll(first, UNIT_CH, axis=1)
        b = pltpu.roll(second, UNIT_CH, axis=1) if keep_low else second
        return jnp.where(low_half, a, b)

    lhs = []
    for n in range(N_UNITS):
        q, low = n // 2, n % 2 == 0
        cols = [halves(u_step[2 * d][:, q * LANES:(q + 1) * LANES],
                       u_step[2 * d + 1][:, q * LANES:(q + 1) * LANES], low)
                for d in range(SSM_BLOCK // 2)]
        lhs.append(jnp.concatenate(cols, axis=1).astype(BF16))
        bu_ref[:, n * UNIT_LANES:(n + 1) * UNIT_LANES] = jnp.dot(
            lhs[n], bmat_ref[n], preferred_element_type=F32)

    for q in range(N_SLABS):
        n_tp = SLAB // TILE_PAIR
        a_re = [lam_ref[:, q * SLAB + j * TILE_PAIR:q * SLAB + j * TILE_PAIR + LANES]
                for j in range(n_tp)]
        a_im = [lam_ref[:, q * SLAB + j * TILE_PAIR + LANES:q * SLAB + (j + 1) * TILE_PAIR]
                for j in range(n_tp)]
        init = carry_ref[:, q * SLAB:(q + 1) * SLAB]
        s = [init[:, i * LANES:(i + 1) * LANES] for i in range(2 * n_tp)]
        for m in range(blocks):
            r0 = m * batch
            st_ref[r0:r0 + batch, q * SLAB:(q + 1) * SLAB] = jnp.concatenate(s, axis=1).astype(BF16)
            bu = bu_ref[r0:r0 + batch, q * SLAB:(q + 1) * SLAB]
            new = []
            for j in range(n_tp):
                s_re, s_im = s[2 * j], s[2 * j + 1]
                b_re = bu[:, j * TILE_PAIR:j * TILE_PAIR + LANES]
                b_im = bu[:, j * TILE_PAIR + LANES:(j + 1) * TILE_PAIR]
                new.append(a_re[j] * s_re - a_im[j] * s_im + b_re)
                new.append(a_re[j] * s_im + a_im[j] * s_re + b_im)
            s = new
        carry_ref[:, q * SLAB:(q + 1) * SLAB] = jnp.concatenate(s, axis=1)

    yb = [jnp.dot(st_ref[:, n * UNIT_LANES:(n + 1) * UNIT_LANES], cmat_ref[n],
                  preferred_element_type=F32)
          + jnp.dot(lhs[n], kmat_ref[n], preferred_element_type=F32) for n in range(N_UNITS)]
    y_step = []
    for j in range(SSM_BLOCK):
        d, low = j // 2, j % 2 == 0
        y_step.append(jnp.concatenate(
            [halves(yb[2 * q][:, d * LANES:(d + 1) * LANES],
                    yb[2 * q + 1][:, d * LANES:(d + 1) * LANES], low)
             for q in range(SSM_WIDTH // LANES)], axis=1))
    y = jnp.concatenate([y_step[j][m * batch:(m + 1) * batch]
                         for m in range(blocks) for j in range(SSM_BLOCK)], axis=0)

    y = y + dskip_ref[...] * u
    y = jax.nn.gelu(y, approximate=True)
    y = y * jax.nn.sigmoid(jnp.dot(y.astype(BF16), wglu_ref[...], preferred_element_type=F32))

    ext = jnp.concatenate([hist_ref[...], zp], axis=0)
    hist_ref[...] = zp[rows - hist_rows:, :]
    t_idx = chunk * steps + lax.shift_right_logical(
        lax.broadcasted_iota(jnp.int32, (rows, LANES), 0), int(math.log2(batch)))
    pooled = []
    for gi, win in enumerate(POOL_WINDOWS):
        col = ext[:, gi * LANES:(gi + 1) * LANES]
        off = 0
        span = 1
        while span < win:
            sh = span * batch
            col = col[sh:, :] + col[:-sh, :]
            off += sh
            span *= 2
        wsum = col[hist_rows - off:hist_rows - off + rows, :]
        count = jnp.minimum(t_idx + 1, win).astype(F32)
        pooled.append(wsum / count - zp[:, gi * LANES:(gi + 1) * LANES])
    pooled = jnp.concatenate(pooled, axis=1).astype(BF16)
    yp = jnp.concatenate(
        [jnp.dot(pooled[:, k * TILE_PAIR:(k + 1) * TILE_PAIR], wpool_ref[k],
                 preferred_element_type=F32) for k in range(2)], axis=1)
    yp = yp * pscale_ref[...]

    mix = jnp.concatenate([y, yp], axis=1).astype(BF16)
    o_ref[...] = h + jnp.dot(mix, wout_ref[...], preferred_element_type=F32)


def _mixer(h, layer, g, win, bmat, lam, cmat, kmat, dskip, wglu, wpool, pscale, wout, batch):
    n = h.shape[0]
    rows = MIX_STEPS * batch
    return pl.pallas_call(
        functools.partial(_mixer_kernel, batch=batch),
        grid=(n // rows,),
        in_specs=[
            _row_spec(rows, D_MODEL),
            _layer_spec((1, D_MODEL), layer),
            _layer_spec((D_MODEL, D_MODEL), layer),
            _layer_spec((N_UNITS, SSM_BLOCK * UNIT_CH, UNIT_LANES), layer),
            _layer_spec((1, STATE_LANES), layer),
            _layer_spec((N_UNITS, UNIT_LANES, SSM_BLOCK * UNIT_CH), layer),
            _layer_spec((N_UNITS, SSM_BLOCK * UNIT_CH, SSM_BLOCK * UNIT_CH), layer),
            _layer_spec((1, SSM_WIDTH), layer),
            _layer_spec((SSM_WIDTH, SSM_WIDTH), layer),
            _layer_spec((2, TILE_PAIR, TILE_PAIR), layer),
            _layer_spec((1, POOL_WIDTH), layer),
            _layer_spec((D_MODEL, D_MODEL), layer),
        ],
        out_specs=_row_spec(rows, D_MODEL),
        out_shape=jax.ShapeDtypeStruct((n, D_MODEL), F32),
        scratch_shapes=[
            pltpu.VMEM((rows // SSM_BLOCK, STATE_LANES), F32),
            pltpu.VMEM((rows // SSM_BLOCK, STATE_LANES), BF16),
            pltpu.VMEM((batch, STATE_LANES), F32),
            pltpu.VMEM((POOL_HIST_STEPS * batch, POOL_WIDTH), F32),
        ],
        compiler_params=pltpu.CompilerParams(
            dimension_semantics=("arbitrary",), vmem_limit_bytes=VMEM_LIMIT),
        name="mixer",
    )(h, g, win, bmat, lam, cmat, kmat, dskip, wglu, wpool, pscale, wout)


def _ssm_params(lam_re, lam_im, log_dt, b_re, b_im, c_re, c_im):
    dt = jnp.exp(log_dt)[:, None]
    mag = jnp.exp(lam_re * dt)
    ab_re = mag * jnp.cos(lam_im * dt)
    ab_im = mag * jnp.sin(lam_im * dt)
    den = lam_re * lam_re + lam_im * lam_im
    f_re = ((ab_re - 1.0) * lam_re + ab_im * lam_im) / den
    f_im = (ab_im * lam_re - (ab_re - 1.0) * lam_im) / den
    bb_re = f_re[..., None] * b_re - f_im[..., None] * b_im
    bb_im = f_re[..., None] * b_im + f_im[..., None] * b_re

    kk = SSM_BLOCK
    ug = UNIT_CH // SSM_GROUP_CH
    bh = kk * SSM_GROUP_CH
    d = jnp.arange(kk + 1, dtype=F32)[:, None, None]
    pw_mag = jnp.exp(d * (lam_re * dt))
    pw_re = pw_mag * jnp.cos(d * (lam_im * dt))
    pw_im = pw_mag * jnp.sin(d * (lam_im * dt))

    lbd_re = pw_re[:kk, :, :, None] * bb_re - pw_im[:kk, :, :, None] * bb_im
    lbd_im = pw_re[:kk, :, :, None] * bb_im + pw_im[:kk, :, :, None] * bb_re
    d_in = jnp.stack([lbd_re[::-1], lbd_im[::-1]], axis=0)
    d_in = d_in.transpose(2, 1, 4, 0, 3).reshape(N_UNITS, ug, bh, 2 * SSM_STATE)
    cl_re = c_re * pw_re[1:, :, None, :] - c_im * pw_im[1:, :, None, :]
    cl_im = c_re * pw_im[1:, :, None, :] + c_im * pw_re[1:, :, None, :]
    d_out = jnp.stack([cl_re, -cl_im], axis=0)
    d_out = d_out.transpose(2, 0, 4, 1, 3).reshape(N_UNITS, ug, 2 * SSM_STATE, bh)
    ein = functools.partial(jnp.einsum, 'gop,dgpi->dgio', precision=lax.Precision.HIGHEST)
    kd = ein(c_re, lbd_re) - ein(c_im, lbd_im)
    kd = jnp.concatenate([kd, jnp.zeros_like(kd[:1])], axis=0)
    lag = np.arange(kk)[None, :] - np.arange(kk)[:, None]
    toep = kd[np.where(lag >= 0, lag, kk)]
    toep = toep.transpose(2, 0, 3, 1, 4).reshape(N_UNITS, ug, bh, bh)

    row = np.arange(kk * UNIT_CH)
    row_g = row // SSM_GROUP_CH % ug
    row_bh = row // UNIT_CH * SSM_GROUP_CH + row % SSM_GROUP_CH
    lane = np.arange(UNIT_LANES)
    lane_g = 2 * (lane // TILE_PAIR) + lane // SSM_STATE % 2
    lane_rp = (lane // LANES % 2) * SSM_STATE + lane % SSM_STATE
    grp = np.arange(ug)[:, None, None]
    lsel = jnp.asarray((row_g[None, :, None] == grp)
                       & (row_bh[None, :, None] == np.arange(bh)[None, None, :]), BF16)
    rsel = jnp.asarray((lane_g[None, None, :] == grp)
                       & (lane_rp[None, None, :] == np.arange(2 * SSM_STATE)[None, :, None]), BF16)

    def place(blocks, left, right_t):
        t = jnp.einsum('ngkl,gbl->ngkb', blocks.astype(BF16), right_t,
                       preferred_element_type=F32).astype(BF16)
        return jnp.einsum('gak,ngkb->nab', left, t, preferred_element_type=F32).astype(BF16)

    rsel_t = rsel.transpose(0, 2, 1)
    bmat = place(d_in, lsel, rsel_t)
    cmat = place(d_out, rsel_t, lsel)
    kmat = place(toep, lsel, lsel)

    lam_lanes = jnp.stack([pw_re[kk].reshape(SSM_GROUPS // 2, 2, SSM_STATE),
                           pw_im[kk].reshape(SSM_GROUPS // 2, 2, SSM_STATE)], axis=1)
    lam_lanes = lam_lanes.reshape(1, STATE_LANES)
    return lam_lanes, bmat, cmat, kmat


def _pool_pairs(w_pool):
    zeros = jnp.zeros((POOL_GROUP_CH, POOL_GROUP_CH), w_pool.dtype)
    pairs = [jnp.block([[w_pool[2 * k], zeros], [zeros, w_pool[2 * k + 1]]]) for k in range(2)]
    return jnp.stack(pairs, axis=0).astype(BF16)


def kernel(x, p, ffn1_norm, ffn1_wi, ffn1_wo, mix_norm, w_in, ssm_lambda_re, ssm_lambda_im, ssm_log_dt, ssm_b_re, ssm_b_im, ssm_c_re, ssm_c_im, ssm_d, ssm_w_glu, pool_w, pool_scale, w_out, ffn2_norm, ffn2_wi, ffn2_wo, ple_norm, ple_w_gate, ple_w_proj, final_norm):
    batch, seq, d = x.shape
    depth = p.shape[0]
    rows = lambda v: v.reshape(depth, 1, -1).astype(F32)
    bf = lambda w: w.astype(BF16)

    lam, bmat, cmat, kmat = jax.vmap(_ssm_params)(ssm_lambda_re, ssm_lambda_im, ssm_log_dt,
                                                  ssm_b_re, ssm_b_im, ssm_c_re, ssm_c_im)
    wpool = jax.vmap(_pool_pairs)(pool_w)
    f1 = (rows(ffn1_norm), bf(ffn1_wi), bf(ffn1_wo))
    f2 = (rows(ffn2_norm), bf(ffn2_wi), bf(ffn2_wo))
    mx = (rows(mix_norm), bf(w_in), bmat, lam, cmat, kmat, rows(ssm_d), bf(ssm_w_glu), wpool,
          rows(pool_scale), bf(w_out))
    gt = (p, rows(ple_norm), bf(ple_w_gate), bf(ple_w_proj), final_norm.reshape(1, -1).astype(F32))
    h = x
    for i in range(depth):
        h = _ffn(h, i, batch, *f1, x_natural=(i == 0))
        h = _mixer(h, i, *mx, batch)
        h = _ffn(h, i, batch, *f2, gate_args=gt, final=(i == depth - 1))
    return h
```

```python
import functools
import math

import jax
import jax.numpy as jnp
import numpy as np
from jax import lax
from jax.experimental import pallas as pl
from jax.experimental.pallas import tpu as pltpu

D_MODEL = 1024
D_FF = 2816
PLE_DIM = 256
SSM_WIDTH = 512
POOL_WIDTH = 512
SSM_GROUP_CH = 16
SSM_GROUPS = 32
SSM_STATE = 64
POOL_WINDOWS = (2, 4, 8, 16)
POOL_GROUP_CH = 128
EPS = 1e-6

LANES = 128
SUBLANES = 8
BF16_ROWS = 2 * SUBLANES
STATE_LANES = 2 * SSM_GROUPS * SSM_STATE
SLAB = 1024
N_SLABS = STATE_LANES // SLAB
TILE_PAIR = 2 * LANES
SSM_BLOCK = 4
UNIT_CH = 2 * LANES // SSM_BLOCK
N_UNITS = SSM_WIDTH // UNIT_CH
UNIT_LANES = STATE_LANES // N_UNITS
POOL_HIST_STEPS = 16

FFN_ROWS = 1024
FFN_CHUNK = 256
MIX_STEPS = 64
VMEM_LIMIT = 56 * 1024 * 1024

F32 = jnp.float32
BF16 = jnp.bfloat16


def _inv_rms(x):
    return lax.rsqrt(jnp.mean(x * x, axis=-1, keepdims=True) + EPS)


def _layer_spec(shape, layer):
    nd = len(shape)
    return pl.BlockSpec((None,) + shape, lambda i: (layer,) + (0,) * nd,
                        pipeline_mode=pl.Buffered(1))


def _const_spec(shape):
    nd = len(shape)
    return pl.BlockSpec(shape, lambda i: (0,) * nd, pipeline_mode=pl.Buffered(1))


def _row_spec(rows, cols):
    return pl.BlockSpec((rows, cols), lambda i: (i, 0))


def _slab_pitch(steps):
    return steps + SUBLANES


def _to_time_major(src_ref, slab_ref):
    batch, steps, cols = src_ref.shape
    n_slab = cols // LANES
    pitch = _slab_pitch(steps)
    for b in range(batch):
        for c in range(n_slab):
            slab_ref[c, b * pitch:b * pitch + steps, :] = src_ref[b, :, c * LANES:(c + 1) * LANES]
    return jnp.concatenate(
        [jnp.concatenate([slab_ref[c, pl.ds(t, batch, stride=pitch), :] for c in range(n_slab)],
                         axis=1) for t in range(steps)], axis=0)


def _from_time_major(y, slab_ref, dst_ref):
    batch, steps, cols = dst_ref.shape
    n_slab = cols // LANES
    pitch = _slab_pitch(steps)
    for t in range(steps):
        for c in range(n_slab):
            slab_ref[c, pl.ds(t, batch, stride=pitch), :] = (
                y[t * batch:(t + 1) * batch, c * LANES:(c + 1) * LANES])
    for b in range(batch):
        for c in range(n_slab):
            dst_ref[b, :, c * LANES:(c + 1) * LANES] = slab_ref[c, b * pitch:b * pitch + steps, :]


def _ffn_kernel(*refs, gated, final, x_natural, n_cast):
    refs = list(refs)
    slab_ref = refs.pop() if (gated or x_natural) else None
    a_ref = refs.pop()
    xg_ref = refs.pop()
    cast_dst = [refs.pop() for _ in range(n_cast)][::-1]
    o_ref = refs.pop()
    cast_src = [refs.pop() for _ in range(n_cast)][::-1]
    for src_ref, dst_ref in zip(cast_src, cast_dst):
        dst_ref[...] = src_ref[...].astype(BF16)
    x_ref, g_ref, wi_ref, wo_ref = refs[:4]
    x = _to_time_major(x_ref, slab_ref) if x_natural else x_ref[...]
    xg_ref[...] = (x * g_ref[...]).astype(BF16)
    r = _inv_rms(x)
    for k in range(D_FF // FFN_CHUNK):
        c0 = k * FFN_CHUNK
        xg = xg_ref[...]
        g = r * jnp.dot(xg, wi_ref[:, c0:c0 + FFN_CHUNK], preferred_element_type=F32)
        u = r * jnp.dot(xg, wi_ref[:, D_FF + c0:D_FF + c0 + FFN_CHUNK], preferred_element_type=F32)
        a_ref[:, c0:c0 + FFN_CHUNK] = (g * jax.nn.sigmoid(g) * u).astype(BF16)
    y = x + 0.5 * jnp.dot(a_ref[...], wo_ref[...], preferred_element_type=F32)
    if gated:
        p_ref, pg_ref, wg_ref, wp_ref, fg_ref = refs[4:]
        pe = _to_time_major(p_ref, slab_ref).astype(BF16)
        gate = jax.nn.sigmoid(_inv_rms(y) * jnp.dot(
            (y * pg_ref[...]).astype(BF16), wg_ref[...], preferred_element_type=F32))
        y = y + gate * jnp.dot(pe, wp_ref[...], preferred_element_type=F32)
    if final:
        y = y * _inv_rms(y) * fg_ref[...]
        _from_time_major(y, slab_ref, o_ref)
    else:
        o_ref[...] = y


def _cast_specs(stacked, layer, n_steps):
    _, n_rows, n_cols = stacked.shape
    block_rows = next(r for r in range(BF16_ROWS, n_rows + 1, BF16_ROWS)
                      if n_rows % r == 0 and n_rows // r <= n_steps)
    last = n_rows // block_rows - 1
    src = pl.BlockSpec((None, block_rows, n_cols), lambda i: (layer, jnp.minimum(i, last), 0))
    dst = pl.BlockSpec((block_rows, n_cols), lambda i: (jnp.minimum(i, last), 0))
    return src, dst, jax.ShapeDtypeStruct((n_rows, n_cols), BF16)


def _ffn(h, layer, batch, g, wi, wo, gate_args=None, final=False, x_natural=False, cast=()):
    steps = FFN_ROWS // batch
    natural_spec = pl.BlockSpec((batch, steps, D_MODEL), lambda i: (0, i, 0))
    if x_natural:
        n = h.shape[0] * h.shape[1]
        x_spec = natural_spec
    else:
        n = h.shape[0]
        x_spec = _row_spec(FFN_ROWS, D_MODEL)
    n_steps = n // FFN_ROWS
    in_specs = [
        x_spec,
        _layer_spec((1, D_MODEL), layer),
        _const_spec((D_MODEL, 2 * D_FF)),
        _const_spec((D_FF, D_MODEL)),
    ]
    args = [h, g, wi, wo]
    scratch = [pltpu.VMEM((FFN_ROWS, D_MODEL), BF16),
               pltpu.VMEM((FFN_ROWS, D_FF), BF16)]
    gated = gate_args is not None
    assert gated or not final
    if gated:
        in_specs += [
            pl.BlockSpec((None, batch, steps, PLE_DIM), lambda i: (layer, 0, i, 0)),
            _layer_spec((1, D_MODEL), layer),
            _layer_spec((D_MODEL, D_MODEL), layer),
            _layer_spec((PLE_DIM, D_MODEL), layer),
            _const_spec((1, D_MODEL)),
        ]
        args += list(gate_args)
    if gated or x_natural:
        n_slab = (D_MODEL if (final or x_natural) else PLE_DIM) // LANES
        scratch.append(pltpu.VMEM((n_slab, batch * _slab_pitch(steps), LANES), F32))
    if final:
        out_specs = [natural_spec]
        out_shape = [jax.ShapeDtypeStruct((batch, n // batch, D_MODEL), F32)]
    else:
        out_specs = [_row_spec(FFN_ROWS, D_MODEL)]
        out_shape = [jax.ShapeDtypeStruct((n, D_MODEL), F32)]
    for stacked, src_layer in cast:
        src, dst, shape = _cast_specs(stacked, src_layer, n_steps)
        in_specs.append(src)
        args.append(stacked)
        out_specs.append(dst)
        out_shape.append(shape)
    return pl.pallas_call(
        functools.partial(_ffn_kernel, gated=gated, final=final, x_natural=x_natural,
                          n_cast=len(cast)),
        grid=(n_steps,),
        in_specs=in_specs,
        out_specs=out_specs,
        out_shape=out_shape,
        scratch_shapes=scratch,
        compiler_params=pltpu.CompilerParams(
            dimension_semantics=("arbitrary",), vmem_limit_bytes=VMEM_LIMIT),
        name="ffn_gate" if gated else "ffn",
    )(*args)


def _mixer_kernel(h_ref, g_ref, win_ref, bmat_ref, lam_ref, cmat_ref, kmat_ref, dskip_ref, wglu_ref,
                  wpool_ref, pscale_ref, wout_ref, o_ref,
                  bu_ref, st_ref, carry_ref, hist_ref, *, batch):
    rows = h_ref.shape[0]
    steps = rows // batch
    hist_rows = POOL_HIST_STEPS * batch
    chunk = pl.program_id(0)

    @pl.when(chunk == 0)
    def _():
        carry_ref[...] = jnp.zeros_like(carry_ref)
        hist_ref[...] = jnp.zeros_like(hist_ref)

    h = h_ref[...]
    z = _inv_rms(h) * jnp.dot((h * g_ref[...]).astype(BF16), win_ref[...],
                              preferred_element_type=F32)
    u = z[:, :SSM_WIDTH]
    zp = z[:, SSM_WIDTH:]

    blocks = steps // SSM_BLOCK
    brows = blocks * batch
    u_step = [jnp.concatenate([u[(SSM_BLOCK * m + i) * batch:(SSM_BLOCK * m + i + 1) * batch]
                               for m in range(blocks)], axis=0) for i in range(SSM_BLOCK)]
    low_half = lax.broadcasted_iota(jnp.int32, (brows, LANES), 1) < UNIT_CH

    def halves(first, second, keep_low):
        a = first if keep_low else pltpu.roll(first, UNIT_CH, axis=1)
        b = pltpu.roll(second, UNIT_CH, axis=1) if keep_low else second
        return jnp.where(low_half, a, b)

    lhs = []
    for n in range(N_UNITS):
        q, low = n // 2, n % 2 == 0
        cols = [halves(u_step[2 * d][:, q * LANES:(q + 1) * LANES],
                       u_step[2 * d + 1][:, q * LANES:(q + 1) * LANES], low)
                for d in range(SSM_BLOCK // 2)]
        lhs.append(jnp.concatenate(cols, axis=1).astype(BF16))
        bu_ref[:, n * UNIT_LANES:(n + 1) * UNIT_LANES] = jnp.dot(
            lhs[n], bmat_ref[n], preferred_element_type=F32)

    for q in range(N_SLABS):
        n_tp = SLAB // TILE_PAIR
        a_re = [lam_ref[:, q * SLAB + j * TILE_PAIR:q * SLAB + j * TILE_PAIR + LANES]
                for j in range(n_tp)]
        a_im = [lam_ref[:, q * SLAB + j * TILE_PAIR + LANES:q * SLAB + (j + 1) * TILE_PAIR]
                for j in range(n_tp)]
        init = carry_ref[:, q * SLAB:(q + 1) * SLAB]
        s = [init[:, i * LANES:(i + 1) * LANES] for i in range(2 * n_tp)]
        for m in range(blocks):
            r0 = m * batch
            st_ref[r0:r0 + batch, q * SLAB:(q + 1) * SLAB] = jnp.concatenate(s, axis=1).astype(BF16)
            bu = bu_ref[r0:r0 + batch, q * SLAB:(q + 1) * SLAB]
            new = []
            for j in range(n_tp):
                s_re, s_im = s[2 * j], s[2 * j + 1]
                b_re = bu[:, j * TILE_PAIR:j * TILE_PAIR + LANES]
                b_im = bu[:, j * TILE_PAIR + LANES:(j + 1) * TILE_PAIR]
                new.append(a_re[j] * s_re - a_im[j] * s_im + b_re)
                new.append(a_re[j] * s_im + a_im[j] * s_re + b_im)
            s = new
        carry_ref[:, q * SLAB:(q + 1) * SLAB] = jnp.concatenate(s, axis=1)

    yb = [jnp.dot(st_ref[:, n * UNIT_LANES:(n + 1) * UNIT_LANES], cmat_ref[n],
                  preferred_element_type=F32)
          + jnp.dot(lhs[n], kmat_ref[n], preferred_element_type=F32) for n in range(N_UNITS)]
    y_step = []
    for j in range(SSM_BLOCK):
        d, low = j // 2, j % 2 == 0
        y_step.append(jnp.concatenate(
            [halves(yb[2 * q][:, d * LANES:(d + 1) * LANES],
                    yb[2 * q + 1][:, d * LANES:(d + 1) * LANES], low)
             for q in range(SSM_WIDTH // LANES)], axis=1))
    y = jnp.concatenate([y_step[j][m * batch:(m + 1) * batch]
                         for m in range(blocks) for j in range(SSM_BLOCK)], axis=0)

    y = y + dskip_ref[...] * u
    y = jax.nn.gelu(y, approximate=True)
    y = y * jax.nn.sigmoid(jnp.dot(y.astype(BF16), wglu_ref[...], preferred_element_type=F32))

    ext = jnp.concatenate([hist_ref[...], zp], axis=0)
    hist_ref[...] = zp[rows - hist_rows:, :]
    t_idx = chunk * steps + lax.shift_right_logical(
        lax.broadcasted_iota(jnp.int32, (rows, LANES), 0), int(math.log2(batch)))
    pooled = []
    for gi, win in enumerate(POOL_WINDOWS):
        col = ext[:, gi * LANES:(gi + 1) * LANES]
        off = 0
        span = 1
        while span < win:
            sh = span * batch
            col = col[sh:, :] + col[:-sh, :]
            off += sh
            span *= 2
        wsum = col[hist_rows - off:hist_rows - off + rows, :]
        count = jnp.minimum(t_idx + 1, win).astype(F32)
        pooled.append(wsum / count - zp[:, gi * LANES:(gi + 1) * LANES])
    pooled = jnp.concatenate(pooled, axis=1).astype(BF16)
    yp = jnp.concatenate(
        [jnp.dot(pooled[:, k * TILE_PAIR:(k + 1) * TILE_PAIR], wpool_ref[k],
                 preferred_element_type=F32) for k in range(2)], axis=1)
    yp = yp * pscale_ref[...]

    mix = jnp.concatenate([y, yp], axis=1).astype(BF16)
    o_ref[...] = h + jnp.dot(mix, wout_ref[...], preferred_element_type=F32)


def _mixer(h, layer, g, win, bmat, lam, cmat, kmat, dskip, wglu, wpool, pscale, wout, batch):
    n = h.shape[0]
    rows = MIX_STEPS * batch
    return pl.pallas_call(
        functools.partial(_mixer_kernel, batch=batch),
        grid=(n // rows,),
        in_specs=[
            _row_spec(rows, D_MODEL),
            _layer_spec((1, D_MODEL), layer),
            _layer_spec((D_MODEL, D_MODEL), layer),
            _layer_spec((N_UNITS, SSM_BLOCK * UNIT_CH, UNIT_LANES), layer),
            _layer_spec((1, STATE_LANES), layer),
            _layer_spec((N_UNITS, UNIT_LANES, SSM_BLOCK * UNIT_CH), layer),
            _layer_spec((N_UNITS, SSM_BLOCK * UNIT_CH, SSM_BLOCK * UNIT_CH), layer),
            _layer_spec((1, SSM_WIDTH), layer),
            _layer_spec((SSM_WIDTH, SSM_WIDTH), layer),
            _layer_spec((2, TILE_PAIR, TILE_PAIR), layer),
            _layer_spec((1, POOL_WIDTH), layer),
            _layer_spec((D_MODEL, D_MODEL), layer),
        ],
        out_specs=_row_spec(rows, D_MODEL),
        out_shape=jax.ShapeDtypeStruct((n, D_MODEL), F32),
        scratch_shapes=[
            pltpu.VMEM((rows // SSM_BLOCK, STATE_LANES), F32),
            pltpu.VMEM((rows // SSM_BLOCK, STATE_LANES), BF16),
            pltpu.VMEM((batch, STATE_LANES), F32),
            pltpu.VMEM((POOL_HIST_STEPS * batch, POOL_WIDTH), F32),
        ],
        compiler_params=pltpu.CompilerParams(
            dimension_semantics=("arbitrary",), vmem_limit_bytes=VMEM_LIMIT),
        name="mixer",
    )(h, g, win, bmat, lam, cmat, kmat, dskip, wglu, wpool, pscale, wout)


def _ssm_params(lam_re, lam_im, log_dt, b_re, b_im, c_re, c_im):
    dt = jnp.exp(log_dt)[:, None]
    mag = jnp.exp(lam_re * dt)
    ab_re = mag * jnp.cos(lam_im * dt)
    ab_im = mag * jnp.sin(lam_im * dt)
    den = lam_re * lam_re + lam_im * lam_im
    f_re = ((ab_re - 1.0) * lam_re + ab_im * lam_im) / den
    f_im = (ab_im * lam_re - (ab_re - 1.0) * lam_im) / den
    bb_re = f_re[..., None] * b_re - f_im[..., None] * b_im
    bb_im = f_re[..., None] * b_im + f_im[..., None] * b_re

    kk = SSM_BLOCK
    ug = UNIT_CH // SSM_GROUP_CH
    bh = kk * SSM_GROUP_CH
    d = jnp.arange(kk + 1, dtype=F32)[:, None, None]
    pw_mag = jnp.exp(d * (lam_re * dt))
    pw_re = pw_mag * jnp.cos(d * (lam_im * dt))
    pw_im = pw_mag * jnp.sin(d * (lam_im * dt))

    lbd_re = pw_re[:kk, :, :, None] * bb_re - pw_im[:kk, :, :, None] * bb_im
    lbd_im = pw_re[:kk, :, :, None] * bb_im + pw_im[:kk, :, :, None] * bb_re
    d_in = jnp.stack([lbd_re[::-1], lbd_im[::-1]], axis=0)
    d_in = d_in.transpose(2, 1, 4, 0, 3).reshape(N_UNITS, ug, bh, 2 * SSM_STATE)
    cl_re = c_re * pw_re[1:, :, None, :] - c_im * pw_im[1:, :, None, :]
    cl_im = c_re * pw_im[1:, :, None, :] + c_im * pw_re[1:, :, None, :]
    d_out = jnp.stack([cl_re, -cl_im], axis=0)
    d_out = d_out.transpose(2, 0, 4, 1, 3).reshape(N_UNITS, ug, 2 * SSM_STATE, bh)
    ein = functools.partial(jnp.einsum, 'gop,dgpi->dgio', precision=lax.Precision.HIGHEST)
    kd = ein(c_re, lbd_re) - ein(c_im, lbd_im)
    kd = jnp.concatenate([kd, jnp.zeros_like(kd[:1])], axis=0)
    lag = np.arange(kk)[None, :] - np.arange(kk)[:, None]
    toep = kd[np.where(lag >= 0, lag, kk)]
    toep = toep.transpose(2, 0, 3, 1, 4).reshape(N_UNITS, ug, bh, bh)

    row = np.arange(kk * UNIT_CH)
    row_g = row // SSM_GROUP_CH % ug
    row_bh = row // UNIT_CH * SSM_GROUP_CH + row % SSM_GROUP_CH
    lane = np.arange(UNIT_LANES)
    lane_g = 2 * (lane // TILE_PAIR) + lane // SSM_STATE % 2
    lane_rp = (lane // LANES % 2) * SSM_STATE + lane % SSM_STATE
    grp = np.arange(ug)[:, None, None]
    lsel = jnp.asarray((row_g[None, :, None] == grp)
                       & (row_bh[None, :, None] == np.arange(bh)[None, None, :]), BF16)
    rsel = jnp.asarray((lane_g[None, None, :] == grp)
                       & (lane_rp[None, None, :] == np.arange(2 * SSM_STATE)[None, :, None]), BF16)

    def place(blocks, left, right_t):
        t = jnp.einsum('ngkl,gbl->ngkb', blocks.astype(BF16), right_t,
                       preferred_element_type=F32).astype(BF16)
        return jnp.einsum('gak,ngkb->nab', left, t, preferred_element_type=F32).astype(BF16)

    rsel_t = rsel.transpose(0, 2, 1)
    bmat = place(d_in, lsel, rsel_t)
    cmat = place(d_out, rsel_t, lsel)
    kmat = place(toep, lsel, lsel)

    lam_lanes = jnp.stack([pw_re[kk].reshape(SSM_GROUPS // 2, 2, SSM_STATE),
                           pw_im[kk].reshape(SSM_GROUPS // 2, 2, SSM_STATE)], axis=1)
    lam_lanes = lam_lanes.reshape(1, STATE_LANES)
    return lam_lanes, bmat, cmat, kmat


def _pool_pairs(w_pool):
    zeros = jnp.zeros((POOL_GROUP_CH, POOL_GROUP_CH), w_pool.dtype)
    pairs = [jnp.block([[w_pool[2 * k], zeros], [zeros, w_pool[2 * k + 1]]]) for k in range(2)]
    return jnp.stack(pairs, axis=0).astype(BF16)


def kernel(x, p, ffn1_norm, ffn1_wi, ffn1_wo, mix_norm, w_in, ssm_lambda_re, ssm_lambda_im, ssm_log_dt, ssm_b_re, ssm_b_im, ssm_c_re, ssm_c_im, ssm_d, ssm_w_glu, pool_w, pool_scale, w_out, ffn2_norm, ffn2_wi, ffn2_wo, ple_norm, ple_w_gate, ple_w_proj, final_norm):
    batch, seq, d = x.shape
    depth = p.shape[0]
    rows = lambda v: v.reshape(depth, 1, -1).astype(F32)
    bf = lambda w: w.astype(BF16)

    lam, bmat, cmat, kmat = jax.vmap(_ssm_params)(ssm_lambda_re, ssm_lambda_im, ssm_log_dt,
                                                  ssm_b_re, ssm_b_im, ssm_c_re, ssm_c_im)
    wpool = jax.vmap(_pool_pairs)(pool_w)
    mx = (rows(mix_norm), bf(w_in), bmat, lam, cmat, kmat, rows(ssm_d), bf(ssm_w_glu), wpool,
          rows(pool_scale), bf(w_out))
    gt = (p, rows(ple_norm), bf(ple_w_gate), bf(ple_w_proj), final_norm.reshape(1, -1).astype(F32))
    g1, g2 = rows(ffn1_norm), rows(ffn2_norm)
    h = x
    wi, wo = bf(ffn1_wi[0]), bf(ffn1_wo[0])
    for i in range(depth):
        last = i == depth - 1
        h, wi, wo = _ffn(h, i, batch, g1, wi, wo, x_natural=(i == 0),
                         cast=((ffn2_wi, i), (ffn2_wo, i)))
        h = _mixer(h, i, *mx, batch)
        h, *nxt = _ffn(h, i, batch, g2, wi, wo, gate_args=gt, final=last,
                       cast=() if last else ((ffn1_wi, i + 1), (ffn1_wo, i + 1)))
        if not last:
            wi, wo = nxt
    return h
```

```python
import functools
import math

import jax
import jax.numpy as jnp
import numpy as np
from jax import lax
from jax.experimental import pallas as pl
from jax.experimental.pallas import tpu as pltpu

D_MODEL = 1024
D_FF = 2816
PLE_DIM = 256
SSM_WIDTH = 512
POOL_WIDTH = 512
SSM_GROUP_CH = 16
SSM_GROUPS = 32
SSM_STATE = 64
POOL_WINDOWS = (2, 4, 8, 16)
POOL_GROUP_CH = 128
EPS = 1e-6

LANES = 128
SUBLANES = 8
BF16_ROWS = 2 * SUBLANES
STATE_LANES = 2 * SSM_GROUPS * SSM_STATE
SLAB = 1024
N_SLABS = STATE_LANES // SLAB
TILE_PAIR = 2 * LANES
SSM_BLOCK = 4
UNIT_CH = 2 * LANES // SSM_BLOCK
N_UNITS = SSM_WIDTH // UNIT_CH
UNIT_LANES = STATE_LANES // N_UNITS
POOL_HIST_STEPS = 16

FFN_ROWS = 1024
FFN_CHUNK = 256
MIX_STEPS = 64
VMEM_LIMIT = 56 * 1024 * 1024

F32 = jnp.float32
BF16 = jnp.bfloat16


def _inv_rms(x):
    return lax.rsqrt(jnp.mean(x * x, axis=-1, keepdims=True) + EPS)


def _layer_spec(shape, layer):
    nd = len(shape)
    return pl.BlockSpec((None,) + shape, lambda i: (layer,) + (0,) * nd,
                        pipeline_mode=pl.Buffered(1))


def _const_spec(shape):
    nd = len(shape)
    return pl.BlockSpec(shape, lambda i: (0,) * nd, pipeline_mode=pl.Buffered(1))


def _row_spec(rows, cols):
    return pl.BlockSpec((rows, cols), lambda i: (i, 0))


def _slab_pitch(steps):
    return steps + SUBLANES


def _to_time_major(src_ref, slab_ref):
    batch, steps, cols = src_ref.shape
    n_slab = cols // LANES
    pitch = _slab_pitch(steps)
    for b in range(batch):
        for c in range(n_slab):
            slab_ref[c, b * pitch:b * pitch + steps, :] = src_ref[b, :, c * LANES:(c + 1) * LANES]
    return jnp.concatenate(
        [jnp.concatenate([slab_ref[c, pl.ds(t, batch, stride=pitch), :] for c in range(n_slab)],
                         axis=1) for t in range(steps)], axis=0)


def _from_time_major(y, slab_ref, dst_ref):
    batch, steps, cols = dst_ref.shape
    n_slab = cols // LANES
    pitch = _slab_pitch(steps)
    for t in range(steps):
        for c in range(n_slab):
            slab_ref[c, pl.ds(t, batch, stride=pitch), :] = (
                y[t * batch:(t + 1) * batch, c * LANES:(c + 1) * LANES])
    for b in range(batch):
        for c in range(n_slab):
            dst_ref[b, :, c * LANES:(c + 1) * LANES] = slab_ref[c, b * pitch:b * pitch + steps, :]


def _ffn_kernel(*refs, gated, final, x_natural, n_cast):
    refs = list(refs)
    slab_ref = refs.pop() if (gated or x_natural) else None
    a_ref = refs.pop()
    xg_ref = refs.pop()
    cast_dst = [refs.pop() for _ in range(n_cast)][::-1]
    o_ref = refs.pop()
    cast_src = [refs.pop() for _ in range(n_cast)][::-1]
    for src_ref, dst_ref in zip(cast_src, cast_dst):
        dst_ref[...] = src_ref[...].astype(BF16)
    x_ref, g_ref, wi_ref, wo_ref = refs[:4]
    x = _to_time_major(x_ref, slab_ref) if x_natural else x_ref[...]
    xg_ref[...] = (x * g_ref[...]).astype(BF16)
    r = _inv_rms(x)
    for k in range(D_FF // FFN_CHUNK):
        c0 = k * FFN_CHUNK
        xg = xg_ref[...]
        g = r * jnp.dot(xg, wi_ref[:, c0:c0 + FFN_CHUNK], preferred_element_type=F32)
        u = r * jnp.dot(xg, wi_ref[:, D_FF + c0:D_FF + c0 + FFN_CHUNK], preferred_element_type=F32)
        a_ref[:, c0:c0 + FFN_CHUNK] = (g * jax.nn.sigmoid(g) * u).astype(BF16)
    y = x + 0.5 * jnp.dot(a_ref[...], wo_ref[...], preferred_element_type=F32)
    if gated:
        p_ref, pg_ref, wg_ref, wp_ref, fg_ref = refs[4:]
        pe = _to_time_major(p_ref, slab_ref).astype(BF16)
        gate = jax.nn.sigmoid(_inv_rms(y) * jnp.dot(
            (y * pg_ref[...]).astype(BF16), wg_ref[...], preferred_element_type=F32))
        y = y + gate * jnp.dot(pe, wp_ref[...], preferred_element_type=F32)
    if final:
        y = y * _inv_rms(y) * fg_ref[...]
        _from_time_major(y, slab_ref, o_ref)
    else:
        o_ref[...] = y


def _cast_specs(stacked, layer, n_steps):
    _, n_rows, n_cols = stacked.shape
    block_rows = next(r for r in range(BF16_ROWS, n_rows + 1, BF16_ROWS)
                      if n_rows % r == 0 and n_rows // r <= n_steps)
    last = n_rows // block_rows - 1
    src = pl.BlockSpec((None, block_rows, n_cols), lambda i: (layer, jnp.minimum(i, last), 0))
    dst = pl.BlockSpec((block_rows, n_cols), lambda i: (jnp.minimum(i, last), 0))
    return src, dst, jax.ShapeDtypeStruct((n_rows, n_cols), BF16)


def _ffn(h, layer, batch, g, wi, wo, gate_args=None, final=False, x_natural=False, cast=()):
    steps = FFN_ROWS // batch
    natural_spec = pl.BlockSpec((batch, steps, D_MODEL), lambda i: (0, i, 0))
    if x_natural:
        n = h.shape[0] * h.shape[1]
        x_spec = natural_spec
    else:
        n = h.shape[0]
        x_spec = _row_spec(FFN_ROWS, D_MODEL)
    n_steps = n // FFN_ROWS
    in_specs = [
        x_spec,
        _layer_spec((1, D_MODEL), layer),
        _const_spec((D_MODEL, 2 * D_FF)),
        _const_spec((D_FF, D_MODEL)),
    ]
    args = [h, g, wi, wo]
    scratch = [pltpu.VMEM((FFN_ROWS, D_MODEL), BF16),
               pltpu.VMEM((FFN_ROWS, D_FF), BF16)]
    gated = gate_args is not None
    assert gated or not final
    if gated:
        in_specs += [
            pl.BlockSpec((None, batch, steps, PLE_DIM), lambda i: (layer, 0, i, 0)),
            _layer_spec((1, D_MODEL), layer),
            _layer_spec((D_MODEL, D_MODEL), layer),
            _layer_spec((PLE_DIM, D_MODEL), layer),
            _const_spec((1, D_MODEL)),
        ]
        args += list(gate_args)
    if gated or x_natural:
        n_slab = (D_MODEL if (final or x_natural) else PLE_DIM) // LANES
        scratch.append(pltpu.VMEM((n_slab, batch * _slab_pitch(steps), LANES), F32))
    if final:
        out_specs = [natural_spec]
        out_shape = [jax.ShapeDtypeStruct((batch, n // batch, D_MODEL), F32)]
    else:
        out_specs = [_row_spec(FFN_ROWS, D_MODEL)]
        out_shape = [jax.ShapeDtypeStruct((n, D_MODEL), F32)]
    for stacked, src_layer in cast:
        src, dst, shape = _cast_specs(stacked, src_layer, n_steps)
        in_specs.append(src)
        args.append(stacked)
        out_specs.append(dst)
        out_shape.append(shape)
    return pl.pallas_call(
        functools.partial(_ffn_kernel, gated=gated, final=final, x_natural=x_natural,
                          n_cast=len(cast)),
        grid=(n_steps,),
        in_specs=in_specs,
        out_specs=out_specs,
        out_shape=out_shape,
        scratch_shapes=scratch,
        compiler_params=pltpu.CompilerParams(
            dimension_semantics=("arbitrary",), vmem_limit_bytes=VMEM_LIMIT),
        name="ffn_gate" if gated else "ffn",
    )(*args)


def _mixer_kernel(h_ref, g_ref, win_ref, bmat_ref, lam_ref, cmat_ref, kmat_ref, dskip_ref, wglu_ref,
                  wpool_ref, pscale_ref, wout_ref, o_ref,
                  bu_ref, st_ref, carry_ref, hist_ref, *, batch):
    rows = h_ref.shape[0]
    steps = rows // batch
    hist_rows = POOL_HIST_STEPS * batch
    chunk = pl.program_id(0)

    @pl.when(chunk == 0)
    def _():
        carry_ref[...] = jnp.zeros_like(carry_ref)
        hist_ref[...] = jnp.zeros_like(hist_ref)

    h = h_ref[...]
    z = _inv_rms(h) * jnp.dot((h * g_ref[...]).astype(BF16), win_ref[...],
                              preferred_element_type=F32)
    u = z[:, :SSM_WIDTH]
    zp = z[:, SSM_WIDTH:]

    blocks = steps // SSM_BLOCK
    brows = blocks * batch
    u_step = [jnp.concatenate([u[(SSM_BLOCK * m + i) * batch:(SSM_BLOCK * m + i + 1) * batch]
                               for m in range(blocks)], axis=0) for i in range(SSM_BLOCK)]
    low_half = lax.broadcasted_iota(jnp.int32, (brows, LANES), 1) < UNIT_CH

    def halves(first, second, keep_low):
        a = first if keep_low else pltpu.roll(first, UNIT_CH, axis=1)
        b = pltpu.roll(second, UNIT_CH, axis=1) if keep_low else second
        return jnp.where(low_half, a, b)

    lhs = []
    for n in range(N_UNITS):
        q, low = n // 2, n % 2 == 0
        cols = [halves(u_step[2 * d][:, q * LANES:(q + 1) * LANES],
                       u_step[2 * d + 1][:, q * LANES:(q + 1) * LANES], low)
                for d in range(SSM_BLOCK // 2)]
        lhs.append(jnp.concatenate(cols, axis=1).astype(BF16))
        bu_ref[:, n * UNIT_LANES:(n + 1) * UNIT_LANES] = jnp.dot(
            lhs[n], bmat_ref[n], preferred_element_type=F32)

    for q in range(N_SLABS):
        n_tp = SLAB // TILE_PAIR
        a_re = [lam_ref[:, q * SLAB + j * TILE_PAIR:q * SLAB + j * TILE_PAIR + LANES]
                for j in range(n_tp)]
        a_im = [lam_ref[:, q * SLAB + j * TILE_PAIR + LANES:q * SLAB + (j + 1) * TILE_PAIR]
                for j in range(n_tp)]
        init = carry_ref[:, q * SLAB:(q + 1) * SLAB]
        s = [init[:, i * LANES:(i + 1) * LANES] for i in range(2 * n_tp)]
        for m in range(blocks):
            r0 = m * batch
            st_ref[r0:r0 + batch, q * SLAB:(q + 1) * SLAB] = jnp.concatenate(s, axis=1).astype(BF16)
            bu = bu_ref[r0:r0 + batch, q * SLAB:(q + 1) * SLAB]
            new = []
            for j in range(n_tp):
                s_re, s_im = s[2 * j], s[2 * j + 1]
                b_re = bu[:, j * TILE_PAIR:j * TILE_PAIR + LANES]
                b_im = bu[:, j * TILE_PAIR + LANES:(j + 1) * TILE_PAIR]
                new.append(a_re[j] * s_re - a_im[j] * s_im + b_re)
                new.append(a_re[j] * s_im + a_im[j] * s_re + b_im)
            s = new
        carry_ref[:, q * SLAB:(q + 1) * SLAB] = jnp.concatenate(s, axis=1)

    yb = [jnp.dot(st_ref[:, n * UNIT_LANES:(n + 1) * UNIT_LANES], cmat_ref[n],
                  preferred_element_type=F32)
          + jnp.dot(lhs[n], kmat_ref[n], preferred_element_type=F32) for n in range(N_UNITS)]
    y_step = []
    for j in range(SSM_BLOCK):
        d, low = j // 2, j % 2 == 0
        y_step.append(jnp.concatenate(
            [halves(yb[2 * q][:, d * LANES:(d + 1) * LANES],
                    yb[2 * q + 1][:, d * LANES:(d + 1) * LANES], low)
             for q in range(SSM_WIDTH // LANES)], axis=1))
    y = jnp.concatenate([y_step[j][m * batch:(m + 1) * batch]
                         for m in range(blocks) for j in range(SSM_BLOCK)], axis=0)

    y = y + dskip_ref[...] * u
    y = jax.nn.gelu(y, approximate=True)
    y = y * jax.nn.sigmoid(jnp.dot(y.astype(BF16), wglu_ref[...], preferred_element_type=F32))

    ext = jnp.concatenate([hist_ref[...], zp], axis=0)
    hist_ref[...] = zp[rows - hist_rows:, :]
    t_idx = chunk * steps + lax.shift_right_logical(
        lax.broadcasted_iota(jnp.int32, (rows, LANES), 0), int(math.log2(batch)))
    pooled = []
    for gi, win in enumerate(POOL_WINDOWS):
        col = ext[:, gi * LANES:(gi + 1) * LANES]
        off = 0
        span = 1
        while span < win:
            sh = span * batch
            col = col[sh:, :] + col[:-sh, :]
            off += sh
            span *= 2
        wsum = col[hist_rows - off:hist_rows - off + rows, :]
        count = jnp.minimum(t_idx + 1, win).astype(F32)
        pooled.append(wsum / count - zp[:, gi * LANES:(gi + 1) * LANES])
    pooled = jnp.concatenate(pooled, axis=1).astype(BF16)
    yp = jnp.concatenate(
        [jnp.dot(pooled[:, k * TILE_PAIR:(k + 1) * TILE_PAIR], wpool_ref[k],
                 preferred_element_type=F32) for k in range(2)], axis=1)
    yp = yp * pscale_ref[...]

    mix = jnp.concatenate([y, yp], axis=1).astype(BF16)
    o_ref[...] = h + jnp.dot(mix, wout_ref[...], preferred_element_type=F32)


def _mixer(h, layer, g, win, bmat, lam, cmat, kmat, dskip, wglu, wpool, pscale, wout, batch):
    n = h.shape[0]
    rows = MIX_STEPS * batch
    return pl.pallas_call(
        functools.partial(_mixer_kernel, batch=batch),
        grid=(n // rows,),
        in_specs=[
            _row_spec(rows, D_MODEL),
            _layer_spec((1, D_MODEL), layer),
            _layer_spec((D_MODEL, D_MODEL), layer),
            _layer_spec((N_UNITS, SSM_BLOCK * UNIT_CH, UNIT_LANES), layer),
            _layer_spec((1, STATE_LANES), layer),
            _layer_spec((N_UNITS, UNIT_LANES, SSM_BLOCK * UNIT_CH), layer),
            _layer_spec((N_UNITS, SSM_BLOCK * UNIT_CH, SSM_BLOCK * UNIT_CH), layer),
            _layer_spec((1, SSM_WIDTH), layer),
            _layer_spec((SSM_WIDTH, SSM_WIDTH), layer),
            _layer_spec((2, TILE_PAIR, TILE_PAIR), layer),
            _layer_spec((1, POOL_WIDTH), layer),
            _layer_spec((D_MODEL, D_MODEL), layer),
        ],
        out_specs=_row_spec(rows, D_MODEL),
        out_shape=jax.ShapeDtypeStruct((n, D_MODEL), F32),
        scratch_shapes=[
            pltpu.VMEM((rows // SSM_BLOCK, STATE_LANES), F32),
            pltpu.VMEM((rows // SSM_BLOCK, STATE_LANES), BF16),
            pltpu.VMEM((batch, STATE_LANES), F32),
            pltpu.VMEM((POOL_HIST_STEPS * batch, POOL_WIDTH), F32),
        ],
        compiler_params=pltpu.CompilerParams(
            dimension_semantics=("arbitrary",), vmem_limit_bytes=VMEM_LIMIT),
        name="mixer",
    )(h, g, win, bmat, lam, cmat, kmat, dskip, wglu, wpool, pscale, wout)


def _ssm_params(lam_re, lam_im, log_dt, b_re, b_im, c_re, c_im):
    dt = jnp.exp(log_dt)[:, None]
    mag = jnp.exp(lam_re * dt)
    ab_re = mag * jnp.cos(lam_im * dt)
    ab_im = mag * jnp.sin(lam_im * dt)
    den = lam_re * lam_re + lam_im * lam_im
    f_re = ((ab_re - 1.0) * lam_re + ab_im * lam_im) / den
    f_im = (ab_im * lam_re - (ab_re - 1.0) * lam_im) / den
    bt_re = b_re.transpose(0, 2, 1)
    bt_im = b_im.transpose(0, 2, 1)
    bb_re = f_re[:, None, :] * bt_re - f_im[:, None, :] * bt_im
    bb_im = f_re[:, None, :] * bt_im + f_im[:, None, :] * bt_re

    kk = SSM_BLOCK
    ug = UNIT_CH // SSM_GROUP_CH
    bh = kk * SSM_GROUP_CH
    d = jnp.arange(kk + 1, dtype=F32)[None, :, None]
    pw_mag = jnp.exp(d * (lam_re * dt)[:, None, :])
    pw_re = pw_mag * jnp.cos(d * (lam_im * dt)[:, None, :])
    pw_im = pw_mag * jnp.sin(d * (lam_im * dt)[:, None, :])
    two = lambda a, b: jnp.concatenate([a, b], axis=-1)
    two_many = lambda xs: jnp.concatenate(xs, axis=-1)

    pin_re, pin_im = pw_re[:, kk - 1::-1], pw_im[:, kk - 1::-1]
    d_in = (two(pin_re, pin_re)[:, :, None, :] * two(bb_re, bb_im)[:, None, :, :]
            + two(-pin_im, pin_im)[:, :, None, :] * two(bb_im, bb_re)[:, None, :, :])
    pout_re, pout_im = pw_re[:, 1:], pw_im[:, 1:]
    d_out = (two(c_re, c_re)[:, None, :, :] * two(pout_re, -pout_im)[:, :, None, :]
             + two(c_im, c_im)[:, None, :, :] * two(-pout_im, -pout_re)[:, :, None, :])
    kd = jnp.einsum('gihk,gok->giho', d_in, two(c_re, -c_im), precision=lax.Precision.HIGHEST)
    zero = jnp.zeros_like(kd[:, 0])
    toep = jnp.stack([two_many([kd[:, kk - 1 - (j - i)] if j >= i else zero for j in range(kk)])
                      for i in range(kk)], axis=1)
    d_in = d_in.reshape(N_UNITS, ug, bh, 2 * SSM_STATE)
    d_out = d_out.reshape(N_UNITS, ug, bh, 2 * SSM_STATE)
    toep = toep.reshape(N_UNITS, ug, bh, bh)

    row = np.arange(kk * UNIT_CH)
    row_g = row // SSM_GROUP_CH % ug
    row_bh = row // UNIT_CH * SSM_GROUP_CH + row % SSM_GROUP_CH
    lane = np.arange(UNIT_LANES)
    lane_g = 2 * (lane // TILE_PAIR) + lane // SSM_STATE % 2
    lane_rp = (lane // LANES % 2) * SSM_STATE + lane % SSM_STATE
    grp = np.arange(ug)[:, None, None]
    lsel = jnp.asarray((row_g[None, :, None] == grp)
                       & (row_bh[None, :, None] == np.arange(bh)[None, None, :]), BF16)
    rsel = jnp.asarray((lane_g[None, None, :] == grp)
                       & (lane_rp[None, None, :] == np.arange(2 * SSM_STATE)[None, :, None]), BF16)

    def place(blocks, left, right_t):
        t = jnp.einsum('ngkl,gbl->ngkb', blocks.astype(BF16), right_t,
                       preferred_element_type=F32).astype(BF16)
        return jnp.einsum('gak,ngkb->nab', left, t, preferred_element_type=F32).astype(BF16)

    rsel_t = rsel.transpose(0, 2, 1)
    bmat = place(d_in, lsel, rsel_t)
    cmat = place(d_out, lsel, rsel_t).transpose(0, 2, 1)
    kmat = place(toep, lsel, lsel)

    lam_lanes = jnp.stack([pw_re[:, kk].reshape(SSM_GROUPS // 2, 2, SSM_STATE),
                           pw_im[:, kk].reshape(SSM_GROUPS // 2, 2, SSM_STATE)], axis=1)
    lam_lanes = lam_lanes.reshape(1, STATE_LANES)
    return lam_lanes, bmat, cmat, kmat


def _pool_pairs(w_pool):
    zeros = jnp.zeros((POOL_GROUP_CH, POOL_GROUP_CH), w_pool.dtype)
    pairs = [jnp.block([[w_pool[2 * k], zeros], [zeros, w_pool[2 * k + 1]]]) for k in range(2)]
    return jnp.stack(pairs, axis=0).astype(BF16)


def kernel(x, p, ffn1_norm, ffn1_wi, ffn1_wo, mix_norm, w_in, ssm_lambda_re, ssm_lambda_im, ssm_log_dt, ssm_b_re, ssm_b_im, ssm_c_re, ssm_c_im, ssm_d, ssm_w_glu, pool_w, pool_scale, w_out, ffn2_norm, ffn2_wi, ffn2_wo, ple_norm, ple_w_gate, ple_w_proj, final_norm):
    batch, seq, d = x.shape
    depth = p.shape[0]
    rows = lambda v: v.reshape(depth, 1, -1).astype(F32)
    bf = lambda w: w.astype(BF16)

    lam, bmat, cmat, kmat = jax.vmap(_ssm_params)(ssm_lambda_re, ssm_lambda_im, ssm_log_dt,
                                                  ssm_b_re, ssm_b_im, ssm_c_re, ssm_c_im)
    wpool = jax.vmap(_pool_pairs)(pool_w)
    mx = (rows(mix_norm), bf(w_in), bmat, lam, cmat, kmat, rows(ssm_d), bf(ssm_w_glu), wpool,
          rows(pool_scale), bf(w_out))
    gt = (p, rows(ple_norm), bf(ple_w_gate), bf(ple_w_proj), final_norm.reshape(1, -1).astype(F32))
    g1, g2 = rows(ffn1_norm), rows(ffn2_norm)
    h = x
    wi, wo = bf(ffn1_wi[0]), bf(ffn1_wo[0])
    for i in range(depth):
        last = i == depth - 1
        h, wi, wo = _ffn(h, i, batch, g1, wi, wo, x_natural=(i == 0),
                         cast=((ffn2_wi, i), (ffn2_wo, i)))
        h = _mixer(h, i, *mx, batch)
        h, *nxt = _ffn(h, i, batch, g2, wi, wo, gate_args=gt, final=last,
                       cast=() if last else ((ffn1_wi, i + 1), (ffn1_wo, i + 1)))
        if not last:
            wi, wo = nxt
    return h
```

```python
import functools
import math

import jax
import jax.numpy as jnp
import numpy as np
from jax import lax
from jax.experimental import pallas as pl
from jax.experimental.pallas import tpu as pltpu

D_MODEL = 1024
D_FF = 2816
PLE_DIM = 256
SSM_WIDTH = 512
POOL_WIDTH = 512
SSM_GROUP_CH = 16
SSM_GROUPS = 32
SSM_STATE = 64
POOL_WINDOWS = (2, 4, 8, 16)
POOL_GROUP_CH = 128
EPS = 1e-6

LANES = 128
SUBLANES = 8
BF16_ROWS = 2 * SUBLANES
STATE_LANES = 2 * SSM_GROUPS * SSM_STATE
SLAB = 1024
N_SLABS = STATE_LANES // SLAB
TILE_PAIR = 2 * LANES
SSM_BLOCK = 4
UNIT_CH = 2 * LANES // SSM_BLOCK
N_UNITS = SSM_WIDTH // UNIT_CH
UNIT_LANES = STATE_LANES // N_UNITS
POOL_HIST_STEPS = 16

FFN_ROWS = 1024
FFN_CHUNK = 256
MIX_STEPS = 64
VMEM_LIMIT = 56 * 1024 * 1024

F32 = jnp.float32
BF16 = jnp.bfloat16


def _inv_rms(x):
    return lax.rsqrt(jnp.mean(x * x, axis=-1, keepdims=True) + EPS)


def _sigmoid(x):
    return 0.5 + 0.5 * jnp.tanh(0.5 * x)


def _gelu_tanh(x):
    c = math.sqrt(2.0 / math.pi)
    t = jnp.tanh(x * (c + (c * 0.044715) * (x * x)))
    return x * (0.5 + 0.5 * t)


def _layer_spec(shape, layer):
    nd = len(shape)
    return pl.BlockSpec((None,) + shape, lambda i: (layer,) + (0,) * nd,
                        pipeline_mode=pl.Buffered(1))


def _const_spec(shape):
    nd = len(shape)
    return pl.BlockSpec(shape, lambda i: (0,) * nd, pipeline_mode=pl.Buffered(1))


def _row_spec(rows, cols):
    return pl.BlockSpec((rows, cols), lambda i: (i, 0))


def _slab_pitch(steps):
    return steps + SUBLANES


def _to_time_major(src_ref, slab_ref):
    batch, steps, cols = src_ref.shape
    n_slab = cols // LANES
    pitch = _slab_pitch(steps)
    for b in range(batch):
        for c in range(n_slab):
            slab_ref[c, b * pitch:b * pitch + steps, :] = src_ref[b, :, c * LANES:(c + 1) * LANES]
    return jnp.concatenate(
        [jnp.concatenate([slab_ref[c, pl.ds(t, batch, stride=pitch), :] for c in range(n_slab)],
                         axis=1) for t in range(steps)], axis=0)


def _from_time_major(y, slab_ref, dst_ref):
    batch, steps, cols = dst_ref.shape
    n_slab = cols // LANES
    pitch = _slab_pitch(steps)
    for t in range(steps):
        for c in range(n_slab):
            slab_ref[c, pl.ds(t, batch, stride=pitch), :] = (
                y[t * batch:(t + 1) * batch, c * LANES:(c + 1) * LANES])
    for b in range(batch):
        for c in range(n_slab):
            dst_ref[b, :, c * LANES:(c + 1) * LANES] = slab_ref[c, b * pitch:b * pitch + steps, :]


def _ffn_kernel(*refs, gated, final, x_natural, n_cast):
    refs = list(refs)
    slab_ref = refs.pop() if (gated or x_natural) else None
    a_ref = refs.pop()
    xg_ref = refs.pop()
    cast_dst = [refs.pop() for _ in range(n_cast)][::-1]
    o_ref = refs.pop()
    cast_src = [refs.pop() for _ in range(n_cast)][::-1]
    for src_ref, dst_ref in zip(cast_src, cast_dst):
        dst_ref[...] = src_ref[...].astype(BF16)
    x_ref, g_ref, wi_ref, wo_ref = refs[:4]
    x = _to_time_major(x_ref, slab_ref) if x_natural else x_ref[...]
    xg_ref[...] = (x * g_ref[...]).astype(BF16)
    r = _inv_rms(x)
    for k in range(D_FF // FFN_CHUNK):
        c0 = k * FFN_CHUNK
        xg = xg_ref[...]
        g = r * jnp.dot(xg, wi_ref[:, c0:c0 + FFN_CHUNK], preferred_element_type=F32)
        u = r * jnp.dot(xg, wi_ref[:, D_FF + c0:D_FF + c0 + FFN_CHUNK], preferred_element_type=F32)
        a_ref[:, c0:c0 + FFN_CHUNK] = (g * _sigmoid(g) * u).astype(BF16)
    y = x + 0.5 * jnp.dot(a_ref[...], wo_ref[...], preferred_element_type=F32)
    if gated:
        p_ref, pg_ref, wg_ref, wp_ref, fg_ref = refs[4:]
        pe = _to_time_major(p_ref, slab_ref).astype(BF16)
        gate = _sigmoid(_inv_rms(y) * jnp.dot(
            (y * pg_ref[...]).astype(BF16), wg_ref[...], preferred_element_type=F32))
        y = y + gate * jnp.dot(pe, wp_ref[...], preferred_element_type=F32)
    if final:
        y = y * _inv_rms(y) * fg_ref[...]
        _from_time_major(y, slab_ref, o_ref)
    else:
        o_ref[...] = y


def _cast_specs(stacked, layer, n_steps):
    _, n_rows, n_cols = stacked.shape
    block_rows = next(r for r in range(BF16_ROWS, n_rows + 1, BF16_ROWS)
                      if n_rows % r == 0 and n_rows // r <= n_steps)
    last = n_rows // block_rows - 1
    src = pl.BlockSpec((None, block_rows, n_cols), lambda i: (layer, jnp.minimum(i, last), 0))
    dst = pl.BlockSpec((block_rows, n_cols), lambda i: (jnp.minimum(i, last), 0))
    return src, dst, jax.ShapeDtypeStruct((n_rows, n_cols), BF16)


def _ffn(h, layer, batch, g, wi, wo, gate_args=None, final=False, x_natural=False, cast=()):
    steps = FFN_ROWS // batch
    natural_spec = pl.BlockSpec((batch, steps, D_MODEL), lambda i: (0, i, 0))
    if x_natural:
        n = h.shape[0] * h.shape[1]
        x_spec = natural_spec
    else:
        n = h.shape[0]
        x_spec = _row_spec(FFN_ROWS, D_MODEL)
    n_steps = n // FFN_ROWS
    in_specs = [
        x_spec,
        _layer_spec((1, D_MODEL), layer),
        _const_spec((D_MODEL, 2 * D_FF)),
        _const_spec((D_FF, D_MODEL)),
    ]
    args = [h, g, wi, wo]
    scratch = [pltpu.VMEM((FFN_ROWS, D_MODEL), BF16),
               pltpu.VMEM((FFN_ROWS, D_FF), BF16)]
    gated = gate_args is not None
    assert gated or not final
    if gated:
        in_specs += [
            pl.BlockSpec((None, batch, steps, PLE_DIM), lambda i: (layer, 0, i, 0)),
            _layer_spec((1, D_MODEL), layer),
            _layer_spec((D_MODEL, D_MODEL), layer),
            _layer_spec((PLE_DIM, D_MODEL), layer),
            _const_spec((1, D_MODEL)),
        ]
        args += list(gate_args)
    if gated or x_natural:
        n_slab = (D_MODEL if (final or x_natural) else PLE_DIM) // LANES
        scratch.append(pltpu.VMEM((n_slab, batch * _slab_pitch(steps), LANES), F32))
    if final:
        out_specs = [natural_spec]
        out_shape = [jax.ShapeDtypeStruct((batch, n // batch, D_MODEL), F32)]
    else:
        out_specs = [_row_spec(FFN_ROWS, D_MODEL)]
        out_shape = [jax.ShapeDtypeStruct((n, D_MODEL), F32)]
    for stacked, src_layer in cast:
        src, dst, shape = _cast_specs(stacked, src_layer, n_steps)
        in_specs.append(src)
        args.append(stacked)
        out_specs.append(dst)
        out_shape.append(shape)
    return pl.pallas_call(
        functools.partial(_ffn_kernel, gated=gated, final=final, x_natural=x_natural,
                          n_cast=len(cast)),
        grid=(n_steps,),
        in_specs=in_specs,
        out_specs=out_specs,
        out_shape=out_shape,
        scratch_shapes=scratch,
        compiler_params=pltpu.CompilerParams(
            dimension_semantics=("arbitrary",), vmem_limit_bytes=VMEM_LIMIT),
        name="ffn_gate" if gated else "ffn",
    )(*args)


def _mixer_kernel(h_ref, g_ref, win_ref, bmat_ref, lam_ref, cmat_ref, kmat_ref, dskip_ref, wglu_ref,
                  wpool_ref, pscale_ref, wout_ref, o_ref,
                  bu_ref, st_ref, carry_ref, hist_ref, *, batch):
    rows = h_ref.shape[0]
    steps = rows // batch
    hist_rows = POOL_HIST_STEPS * batch
    chunk = pl.program_id(0)

    @pl.when(chunk == 0)
    def _():
        carry_ref[...] = jnp.zeros_like(carry_ref)
        hist_ref[...] = jnp.zeros_like(hist_ref)

    h = h_ref[...]
    z = _inv_rms(h) * jnp.dot((h * g_ref[...]).astype(BF16), win_ref[...],
                              preferred_element_type=F32)
    u = z[:, :SSM_WIDTH]
    zp = z[:, SSM_WIDTH:]

    blocks = steps // SSM_BLOCK
    brows = blocks * batch
    u_step = [jnp.concatenate([u[(SSM_BLOCK * m + i) * batch:(SSM_BLOCK * m + i + 1) * batch]
                               for m in range(blocks)], axis=0) for i in range(SSM_BLOCK)]
    low_half = lax.broadcasted_iota(jnp.int32, (brows, LANES), 1) < UNIT_CH

    def halves(first, second, keep_low):
        a = first if keep_low else pltpu.roll(first, UNIT_CH, axis=1)
        b = pltpu.roll(second, UNIT_CH, axis=1) if keep_low else second
        return jnp.where(low_half, a, b)

    lhs = []
    for n in range(N_UNITS):
        q, low = n // 2, n % 2 == 0
        cols = [halves(u_step[2 * d][:, q * LANES:(q + 1) * LANES],
                       u_step[2 * d + 1][:, q * LANES:(q + 1) * LANES], low)
                for d in range(SSM_BLOCK // 2)]
        lhs.append(jnp.concatenate(cols, axis=1).astype(BF16))
        bu_ref[:, n * UNIT_LANES:(n + 1) * UNIT_LANES] = jnp.dot(
            lhs[n], bmat_ref[n], preferred_element_type=F32)

    for q in range(N_SLABS):
        n_tp = SLAB // TILE_PAIR
        a_re = [lam_ref[:, q * SLAB + j * TILE_PAIR:q * SLAB + j * TILE_PAIR + LANES]
                for j in range(n_tp)]
        a_im = [lam_ref[:, q * SLAB + j * TILE_PAIR + LANES:q * SLAB + (j + 1) * TILE_PAIR]
                for j in range(n_tp)]
        init = carry_ref[:, q * SLAB:(q + 1) * SLAB]
        s = [init[:, i * LANES:(i + 1) * LANES] for i in range(2 * n_tp)]
        for m in range(blocks):
            r0 = m * batch
            st_ref[r0:r0 + batch, q * SLAB:(q + 1) * SLAB] = jnp.concatenate(s, axis=1).astype(BF16)
            bu = bu_ref[r0:r0 + batch, q * SLAB:(q + 1) * SLAB]
            new = []
            for j in range(n_tp):
                s_re, s_im = s[2 * j], s[2 * j + 1]
                b_re = bu[:, j * TILE_PAIR:j * TILE_PAIR + LANES]
                b_im = bu[:, j * TILE_PAIR + LANES:(j + 1) * TILE_PAIR]
                new.append(a_re[j] * s_re - a_im[j] * s_im + b_re)
                new.append(a_re[j] * s_im + a_im[j] * s_re + b_im)
            s = new
        carry_ref[:, q * SLAB:(q + 1) * SLAB] = jnp.concatenate(s, axis=1)

    yb = [jnp.dot(st_ref[:, n * UNIT_LANES:(n + 1) * UNIT_LANES], cmat_ref[n],
                  preferred_element_type=F32)
          + jnp.dot(lhs[n], kmat_ref[n], preferred_element_type=F32) for n in range(N_UNITS)]
    y_step = []
    for j in range(SSM_BLOCK):
        d, low = j // 2, j % 2 == 0
        y_step.append(jnp.concatenate(
            [halves(yb[2 * q][:, d * LANES:(d + 1) * LANES],
                    yb[2 * q + 1][:, d * LANES:(d + 1) * LANES], low)
             for q in range(SSM_WIDTH // LANES)], axis=1))
    y = jnp.concatenate([y_step[j][m * batch:(m + 1) * batch]
                         for m in range(blocks) for j in range(SSM_BLOCK)], axis=0)

    y = y + dskip_ref[...] * u
    y = _gelu_tanh(y)
    y = y * _sigmoid(jnp.dot(y.astype(BF16), wglu_ref[...], preferred_element_type=F32))

    ext = jnp.concatenate([hist_ref[...], zp], axis=0)
    hist_ref[...] = zp[rows - hist_rows:, :]
    t_idx = chunk * steps + lax.shift_right_logical(
        lax.broadcasted_iota(jnp.int32, (rows, LANES), 0), int(math.log2(batch)))
    pooled = []
    for gi, win in enumerate(POOL_WINDOWS):
        col = ext[:, gi * LANES:(gi + 1) * LANES]
        off = 0
        span = 1
        while span < win:
            sh = span * batch
            col = col[sh:, :] + col[:-sh, :]
            off += sh
            span *= 2
        wsum = col[hist_rows - off:hist_rows - off + rows, :]
        head = 1.0 / jnp.minimum(t_idx[:hist_rows] + 1, win).astype(F32)
        inv = jnp.concatenate([head, jnp.full((rows - hist_rows, LANES), 1.0 / win, F32)], axis=0)
        pooled.append(wsum * inv - zp[:, gi * LANES:(gi + 1) * LANES])
    pooled = jnp.concatenate(pooled, axis=1).astype(BF16)
    yp = jnp.concatenate(
        [jnp.dot(pooled[:, k * TILE_PAIR:(k + 1) * TILE_PAIR], wpool_ref[k],
                 preferred_element_type=F32) for k in range(2)], axis=1)
    yp = yp * pscale_ref[...]

    mix = jnp.concatenate([y, yp], axis=1).astype(BF16)
    o_ref[...] = h + jnp.dot(mix, wout_ref[...], preferred_element_type=F32)


def _mixer(h, layer, g, win, bmat, lam, cmat, kmat, dskip, wglu, wpool, pscale, wout, batch):
    n = h.shape[0]
    rows = MIX_STEPS * batch
    return pl.pallas_call(
        functools.partial(_mixer_kernel, batch=batch),
        grid=(n // rows,),
        in_specs=[
            _row_spec(rows, D_MODEL),
            _layer_spec((1, D_MODEL), layer),
            _layer_spec((D_MODEL, D_MODEL), layer),
            _layer_spec((N_UNITS, SSM_BLOCK * UNIT_CH, UNIT_LANES), layer),
            _layer_spec((1, STATE_LANES), layer),
            _layer_spec((N_UNITS, UNIT_LANES, SSM_BLOCK * UNIT_CH), layer),
            _layer_spec((N_UNITS, SSM_BLOCK * UNIT_CH, SSM_BLOCK * UNIT_CH), layer),
            _layer_spec((1, SSM_WIDTH), layer),
            _layer_spec((SSM_WIDTH, SSM_WIDTH), layer),
            _layer_spec((2, TILE_PAIR, TILE_PAIR), layer),
            _layer_spec((1, POOL_WIDTH), layer),
            _layer_spec((D_MODEL, D_MODEL), layer),
        ],
        out_specs=_row_spec(rows, D_MODEL),
        out_shape=jax.ShapeDtypeStruct((n, D_MODEL), F32),
        scratch_shapes=[
            pltpu.VMEM((rows // SSM_BLOCK, STATE_LANES), F32),
            pltpu.VMEM((rows // SSM_BLOCK, STATE_LANES), BF16),
            pltpu.VMEM((batch, STATE_LANES), F32),
            pltpu.VMEM((POOL_HIST_STEPS * batch, POOL_WIDTH), F32),
        ],
        compiler_params=pltpu.CompilerParams(
            dimension_semantics=("arbitrary",), vmem_limit_bytes=VMEM_LIMIT),
        name="mixer",
    )(h, g, win, bmat, lam, cmat, kmat, dskip, wglu, wpool, pscale, wout)


def _ssm_params(lam_re, lam_im, log_dt, b_re, b_im, c_re, c_im):
    dt = jnp.exp(log_dt)[:, None]
    mag = jnp.exp(lam_re * dt)
    ab_re = mag * jnp.cos(lam_im * dt)
    ab_im = mag * jnp.sin(lam_im * dt)
    den = lam_re * lam_re + lam_im * lam_im
    f_re = ((ab_re - 1.0) * lam_re + ab_im * lam_im) / den
    f_im = (ab_im * lam_re - (ab_re - 1.0) * lam_im) / den
    bt_re = b_re.transpose(0, 2, 1)
    bt_im = b_im.transpose(0, 2, 1)
    bb_re = f_re[:, None, :] * bt_re - f_im[:, None, :] * bt_im
    bb_im = f_re[:, None, :] * bt_im + f_im[:, None, :] * bt_re

    kk = SSM_BLOCK
    ug = UNIT_CH // SSM_GROUP_CH
    bh = kk * SSM_GROUP_CH
    d = jnp.arange(kk + 1, dtype=F32)[None, :, None]
    pw_mag = jnp.exp(d * (lam_re * dt)[:, None, :])
    pw_re = pw_mag * jnp.cos(d * (lam_im * dt)[:, None, :])
    pw_im = pw_mag * jnp.sin(d * (lam_im * dt)[:, None, :])
    two = lambda a, b: jnp.concatenate([a, b], axis=-1)
    two_many = lambda xs: jnp.concatenate(xs, axis=-1)

    pin_re, pin_im = pw_re[:, kk - 1::-1], pw_im[:, kk - 1::-1]
    d_in = (two(pin_re, pin_re)[:, :, None, :] * two(bb_re, bb_im)[:, None, :, :]
            + two(-pin_im, pin_im)[:, :, None, :] * two(bb_im, bb_re)[:, None, :, :])
    pout_re, pout_im = pw_re[:, 1:], pw_im[:, 1:]
    d_out = (two(c_re, c_re)[:, None, :, :] * two(pout_re, -pout_im)[:, :, None, :]
             + two(c_im, c_im)[:, None, :, :] * two(-pout_im, -pout_re)[:, :, None, :])
    kd = jnp.einsum('gihk,gok->giho', d_in, two(c_re, -c_im), precision=lax.Precision.HIGHEST)
    zero = jnp.zeros_like(kd[:, 0])
    toep = jnp.stack([two_many([kd[:, kk - 1 - (j - i)] if j >= i else zero for j in range(kk)])
                      for i in range(kk)], axis=1)
    d_in = d_in.reshape(N_UNITS, ug, bh, 2 * SSM_STATE)
    d_out = d_out.reshape(N_UNITS, ug, bh, 2 * SSM_STATE)
    toep = toep.reshape(N_UNITS, ug, bh, bh)

    row = np.arange(kk * UNIT_CH)
    row_g = row // SSM_GROUP_CH % ug
    row_bh = row // UNIT_CH * SSM_GROUP_CH + row % SSM_GROUP_CH
    lane = np.arange(UNIT_LANES)
    lane_g = 2 * (lane // TILE_PAIR) + lane // SSM_STATE % 2
    lane_rp = (lane // LANES % 2) * SSM_STATE + lane % SSM_STATE
    grp = np.arange(ug)[:, None, None]
    lsel = jnp.asarray((row_g[None, :, None] == grp)
                       & (row_bh[None, :, None] == np.arange(bh)[None, None, :]), BF16)
    rsel = jnp.asarray((lane_g[None, None, :] == grp)
                       & (lane_rp[None, None, :] == np.arange(2 * SSM_STATE)[None, :, None]), BF16)

    def place(blocks, left, right_t):
        t = jnp.einsum('ngkl,gbl->ngkb', blocks.astype(BF16), right_t,
                       preferred_element_type=F32).astype(BF16)
        return jnp.einsum('gak,ngkb->nab', left, t, preferred_element_type=F32).astype(BF16)

    rsel_t = rsel.transpose(0, 2, 1)
    bmat = place(d_in, lsel, rsel_t)
    cmat = place(d_out, lsel, rsel_t).transpose(0, 2, 1)
    kmat = place(toep, lsel, lsel)

    lam_lanes = jnp.stack([pw_re[:, kk].reshape(SSM_GROUPS // 2, 2, SSM_STATE),
                           pw_im[:, kk].reshape(SSM_GROUPS // 2, 2, SSM_STATE)], axis=1)
    lam_lanes = lam_lanes.reshape(1, STATE_LANES)
    return lam_lanes, bmat, cmat, kmat


def _pool_pairs(w_pool):
    zeros = jnp.zeros((POOL_GROUP_CH, POOL_GROUP_CH), w_pool.dtype)
    pairs = [jnp.block([[w_pool[2 * k], zeros], [zeros, w_pool[2 * k + 1]]]) for k in range(2)]
    return jnp.stack(pairs, axis=0).astype(BF16)


def kernel(x, p, ffn1_norm, ffn1_wi, ffn1_wo, mix_norm, w_in, ssm_lambda_re, ssm_lambda_im, ssm_log_dt, ssm_b_re, ssm_b_im, ssm_c_re, ssm_c_im, ssm_d, ssm_w_glu, pool_w, pool_scale, w_out, ffn2_norm, ffn2_wi, ffn2_wo, ple_norm, ple_w_gate, ple_w_proj, final_norm):
    batch, seq, d = x.shape
    depth = p.shape[0]
    rows = lambda v: v.reshape(depth, 1, -1).astype(F32)
    bf = lambda w: w.astype(BF16)

    lam, bmat, cmat, kmat = jax.vmap(_ssm_params)(ssm_lambda_re, ssm_lambda_im, ssm_log_dt,
                                                  ssm_b_re, ssm_b_im, ssm_c_re, ssm_c_im)
    wpool = jax.vmap(_pool_pairs)(pool_w)
    mx = (rows(mix_norm), bf(w_in), bmat, lam, cmat, kmat, rows(ssm_d), bf(ssm_w_glu), wpool,
          rows(pool_scale), bf(w_out))
    gt = (p, rows(ple_norm), bf(ple_w_gate), bf(ple_w_proj), final_norm.reshape(1, -1).astype(F32))
    g1, g2 = rows(ffn1_norm), rows(ffn2_norm)
    h = x
    wi, wo = bf(ffn1_wi[0]), bf(ffn1_wo[0])
    for i in range(depth):
        last = i == depth - 1
        h, wi, wo = _ffn(h, i, batch, g1, wi, wo, x_natural=(i == 0),
                         cast=((ffn2_wi, i), (ffn2_wo, i)))
        h = _mixer(h, i, *mx, batch)
        h, *nxt = _ffn(h, i, batch, g2, wi, wo, gate_args=gt, final=last,
                       cast=() if last else ((ffn1_wi, i + 1), (ffn1_wo, i + 1)))
        if not last:
            wi, wo = nxt
    return h
```

```python
import functools
import math

import jax
import jax.numpy as jnp
import numpy as np
from jax import lax
from jax.experimental import pallas as pl
from jax.experimental.pallas import tpu as pltpu

D_MODEL = 1024
D_FF = 2816
PLE_DIM = 256
SSM_WIDTH = 512
POOL_WIDTH = 512
SSM_GROUP_CH = 16
SSM_GROUPS = 32
SSM_STATE = 64
POOL_WINDOWS = (2, 4, 8, 16)
POOL_GROUP_CH = 128
EPS = 1e-6

LANES = 128
SUBLANES = 8
BF16_ROWS = 2 * SUBLANES
STATE_LANES = 2 * SSM_GROUPS * SSM_STATE
SLAB = 1024
N_SLABS = STATE_LANES // SLAB
TILE_PAIR = 2 * LANES
SSM_BLOCK = 4
UNIT_CH = 2 * LANES // SSM_BLOCK
N_UNITS = SSM_WIDTH // UNIT_CH
UNIT_LANES = STATE_LANES // N_UNITS
POOL_HIST_STEPS = 16

FFN_ROWS = 1024
FFN_CHUNK = 256
MIX_STEPS = 64
VMEM_LIMIT = 56 * 1024 * 1024

F32 = jnp.float32
BF16 = jnp.bfloat16


def _inv_rms(x):
    return lax.rsqrt(jnp.mean(x * x, axis=-1, keepdims=True) + EPS)


def _gelu_tanh(x):
    c = math.sqrt(2.0 / math.pi)
    t = jnp.tanh(x * (c + (c * 0.044715) * (x * x)))
    return x * (0.5 + 0.5 * t)


def _layer_spec(shape, layer):
    nd = len(shape)
    return pl.BlockSpec((None,) + shape, lambda i: (layer,) + (0,) * nd,
                        pipeline_mode=pl.Buffered(1))


def _const_spec(shape):
    nd = len(shape)
    return pl.BlockSpec(shape, lambda i: (0,) * nd, pipeline_mode=pl.Buffered(1))


def _row_spec(rows, cols):
    return pl.BlockSpec((rows, cols), lambda i: (i, 0))


def _slab_pitch(steps):
    return steps + SUBLANES


def _to_time_major(src_ref, slab_ref):
    batch, steps, cols = src_ref.shape
    n_slab = cols // LANES
    pitch = _slab_pitch(steps)
    for b in range(batch):
        for c in range(n_slab):
            slab_ref[c, b * pitch:b * pitch + steps, :] = src_ref[b, :, c * LANES:(c + 1) * LANES]
    return jnp.concatenate(
        [jnp.concatenate([slab_ref[c, pl.ds(t, batch, stride=pitch), :] for c in range(n_slab)],
                         axis=1) for t in range(steps)], axis=0)


def _from_time_major(y, slab_ref, dst_ref):
    batch, steps, cols = dst_ref.shape
    n_slab = cols // LANES
    pitch = _slab_pitch(steps)
    for t in range(steps):
        for c in range(n_slab):
            slab_ref[c, pl.ds(t, batch, stride=pitch), :] = (
                y[t * batch:(t + 1) * batch, c * LANES:(c + 1) * LANES])
    for b in range(batch):
        for c in range(n_slab):
            dst_ref[b, :, c * LANES:(c + 1) * LANES] = slab_ref[c, b * pitch:b * pitch + steps, :]


def _ffn_kernel(*refs, gated, final, x_natural, n_cast):
    refs = list(refs)
    slab_ref = refs.pop() if (gated or x_natural) else None
    a_ref = refs.pop()
    xg_ref = refs.pop()
    cast_dst = [refs.pop() for _ in range(n_cast)][::-1]
    o_ref = refs.pop()
    cast_src = [refs.pop() for _ in range(n_cast)][::-1]
    for src_ref, dst_ref in zip(cast_src, cast_dst):
        dst_ref[...] = src_ref[...].astype(BF16)
    x_ref, g_ref, wi_ref, wo_ref = refs[:4]
    x = _to_time_major(x_ref, slab_ref) if x_natural else x_ref[...]
    xg_ref[...] = (x * g_ref[...]).astype(BF16)
    r = _inv_rms(x)
    half_r = 0.5 * r
    for k in range(D_FF // FFN_CHUNK):
        c0 = k * FFN_CHUNK
        xg = xg_ref[...]
        hg = half_r * jnp.dot(xg, wi_ref[:, c0:c0 + FFN_CHUNK], preferred_element_type=F32)
        u = r * jnp.dot(xg, wi_ref[:, D_FF + c0:D_FF + c0 + FFN_CHUNK], preferred_element_type=F32)
        a_ref[:, c0:c0 + FFN_CHUNK] = ((hg + hg * jnp.tanh(hg)) * u).astype(BF16)
    y = x + 0.5 * jnp.dot(a_ref[...], wo_ref[...], preferred_element_type=F32)
    if gated:
        p_ref, pg_ref, wg_ref, wp_ref, fg_ref = refs[4:]
        pe = _to_time_major(p_ref, slab_ref).astype(BF16)
        v = _inv_rms(y) * jnp.dot(
            (y * pg_ref[...]).astype(BF16), wg_ref[...], preferred_element_type=F32)
        gate = 0.5 + 0.5 * jnp.tanh(0.5 * v)
        y = y + gate * jnp.dot(pe, wp_ref[...], preferred_element_type=F32)
    if final:
        y = y * _inv_rms(y) * fg_ref[...]
        _from_time_major(y, slab_ref, o_ref)
    else:
        o_ref[...] = y


def _cast_specs(stacked, layer, n_steps):
    _, n_rows, n_cols = stacked.shape
    block_rows = next(r for r in range(BF16_ROWS, n_rows + 1, BF16_ROWS)
                      if n_rows % r == 0 and n_rows // r <= n_steps)
    last = n_rows // block_rows - 1
    src = pl.BlockSpec((None, block_rows, n_cols), lambda i: (layer, jnp.minimum(i, last), 0))
    dst = pl.BlockSpec((block_rows, n_cols), lambda i: (jnp.minimum(i, last), 0))
    return src, dst, jax.ShapeDtypeStruct((n_rows, n_cols), BF16)


def _ffn(h, layer, batch, g, wi, wo, gate_args=None, final=False, x_natural=False, cast=()):
    steps = FFN_ROWS // batch
    natural_spec = pl.BlockSpec((batch, steps, D_MODEL), lambda i: (0, i, 0))
    if x_natural:
        n = h.shape[0] * h.shape[1]
        x_spec = natural_spec
    else:
        n = h.shape[0]
        x_spec = _row_spec(FFN_ROWS, D_MODEL)
    n_steps = n // FFN_ROWS
    in_specs = [
        x_spec,
        _layer_spec((1, D_MODEL), layer),
        _const_spec((D_MODEL, 2 * D_FF)),
        _const_spec((D_FF, D_MODEL)),
    ]
    args = [h, g, wi, wo]
    scratch = [pltpu.VMEM((FFN_ROWS, D_MODEL), BF16),
               pltpu.VMEM((FFN_ROWS, D_FF), BF16)]
    gated = gate_args is not None
    assert gated or not final
    if gated:
        in_specs += [
            pl.BlockSpec((None, batch, steps, PLE_DIM), lambda i: (layer, 0, i, 0)),
            _layer_spec((1, D_MODEL), layer),
            _layer_spec((D_MODEL, D_MODEL), layer),
            _layer_spec((PLE_DIM, D_MODEL), layer),
            _const_spec((1, D_MODEL)),
        ]
        args += list(gate_args)
    if gated or x_natural:
        n_slab = (D_MODEL if (final or x_natural) else PLE_DIM) // LANES
        scratch.append(pltpu.VMEM((n_slab, batch * _slab_pitch(steps), LANES), F32))
    if final:
        out_specs = [natural_spec]
        out_shape = [jax.ShapeDtypeStruct((batch, n // batch, D_MODEL), F32)]
    else:
        out_specs = [_row_spec(FFN_ROWS, D_MODEL)]
        out_shape = [jax.ShapeDtypeStruct((n, D_MODEL), F32)]
    for stacked, src_layer in cast:
        src, dst, shape = _cast_specs(stacked, src_layer, n_steps)
        in_specs.append(src)
        args.append(stacked)
        out_specs.append(dst)
        out_shape.append(shape)
    return pl.pallas_call(
        functools.partial(_ffn_kernel, gated=gated, final=final, x_natural=x_natural,
                          n_cast=len(cast)),
        grid=(n_steps,),
        in_specs=in_specs,
        out_specs=out_specs,
        out_shape=out_shape,
        scratch_shapes=scratch,
        compiler_params=pltpu.CompilerParams(
            dimension_semantics=("arbitrary",), vmem_limit_bytes=VMEM_LIMIT),
        name="ffn_gate" if gated else "ffn",
    )(*args)


def _mixer_kernel(h_ref, g_ref, win_ref, bmat_ref, lam_ref, cmat_ref, kmat_ref, dskip_ref, wglu_ref,
                  wpool_ref, pscale_ref, wout_ref, o_ref,
                  bu_ref, st_ref, carry_ref, hist_ref, *, batch):
    rows = h_ref.shape[0]
    steps = rows // batch
    hist_rows = POOL_HIST_STEPS * batch
    chunk = pl.program_id(0)

    @pl.when(chunk == 0)
    def _():
        carry_ref[...] = jnp.zeros_like(carry_ref)
        hist_ref[...] = jnp.zeros_like(hist_ref)

    h = h_ref[...]
    z = _inv_rms(h) * jnp.dot((h * g_ref[...]).astype(BF16), win_ref[...],
                              preferred_element_type=F32)
    u = z[:, :SSM_WIDTH]
    zp = z[:, SSM_WIDTH:]

    blocks = steps // SSM_BLOCK
    brows = blocks * batch
    u_step = [jnp.concatenate([u[(SSM_BLOCK * m + i) * batch:(SSM_BLOCK * m + i + 1) * batch]
                               for m in range(blocks)], axis=0) for i in range(SSM_BLOCK)]
    low_half = lax.broadcasted_iota(jnp.int32, (brows, LANES), 1) < UNIT_CH

    def halves(first, second, keep_low):
        a = first if keep_low else pltpu.roll(first, UNIT_CH, axis=1)
        b = pltpu.roll(second, UNIT_CH, axis=1) if keep_low else second
        return jnp.where(low_half, a, b)

    lhs = []
    for n in range(N_UNITS):
        q, low = n // 2, n % 2 == 0
        cols = [halves(u_step[2 * d][:, q * LANES:(q + 1) * LANES],
                       u_step[2 * d + 1][:, q * LANES:(q + 1) * LANES], low)
                for d in range(SSM_BLOCK // 2)]
        lhs.append(jnp.concatenate(cols, axis=1).astype(BF16))
        bu_ref[:, n * UNIT_LANES:(n + 1) * UNIT_LANES] = jnp.dot(
            lhs[n], bmat_ref[n], preferred_element_type=F32)

    for q in range(N_SLABS):
        n_tp = SLAB // TILE_PAIR
        a_re = [lam_ref[:, q * SLAB + j * TILE_PAIR:q * SLAB + j * TILE_PAIR + LANES]
                for j in range(n_tp)]
        a_im = [lam_ref[:, q * SLAB + j * TILE_PAIR + LANES:q * SLAB + (j + 1) * TILE_PAIR]
                for j in range(n_tp)]
        init = carry_ref[:, q * SLAB:(q + 1) * SLAB]
        s = [init[:, i * LANES:(i + 1) * LANES] for i in range(2 * n_tp)]
        for m in range(blocks):
            r0 = m * batch
            st_ref[r0:r0 + batch, q * SLAB:(q + 1) * SLAB] = jnp.concatenate(s, axis=1).astype(BF16)
            bu = bu_ref[r0:r0 + batch, q * SLAB:(q + 1) * SLAB]
            new = []
            for j in range(n_tp):
                s_re, s_im = s[2 * j], s[2 * j + 1]
                b_re = bu[:, j * TILE_PAIR:j * TILE_PAIR + LANES]
                b_im = bu[:, j * TILE_PAIR + LANES:(j + 1) * TILE_PAIR]
                new.append(a_re[j] * s_re - a_im[j] * s_im + b_re)
                new.append(a_re[j] * s_im + a_im[j] * s_re + b_im)
            s = new
        carry_ref[:, q * SLAB:(q + 1) * SLAB] = jnp.concatenate(s, axis=1)

    yb = [jnp.dot(st_ref[:, n * UNIT_LANES:(n + 1) * UNIT_LANES], cmat_ref[n],
                  preferred_element_type=F32)
          + jnp.dot(lhs[n], kmat_ref[n], preferred_element_type=F32) for n in range(N_UNITS)]
    y_step = []
    for j in range(SSM_BLOCK):
        d, low = j // 2, j % 2 == 0
        y_step.append(jnp.concatenate(
            [halves(yb[2 * q][:, d * LANES:(d + 1) * LANES],
                    yb[2 * q + 1][:, d * LANES:(d + 1) * LANES], low)
             for q in range(SSM_WIDTH // LANES)], axis=1))
    y = jnp.concatenate([y_step[j][m * batch:(m + 1) * batch]
                         for m in range(blocks) for j in range(SSM_BLOCK)], axis=0)

    y = y + dskip_ref[...] * u
    y = _gelu_tanh(y)
    half_y = 0.5 * y
    y = half_y + half_y * jnp.tanh(
        0.5 * jnp.dot(y.astype(BF16), wglu_ref[...], preferred_element_type=F32))

    ext = jnp.concatenate([hist_ref[...], zp], axis=0)
    hist_ref[...] = zp[rows - hist_rows:, :]
    t_idx = chunk * steps + lax.shift_right_logical(
        lax.broadcasted_iota(jnp.int32, (rows, LANES), 0), int(math.log2(batch)))
    pooled = []
    for gi, win in enumerate(POOL_WINDOWS):
        col = ext[:, gi * LANES:(gi + 1) * LANES]
        off = 0
        span = 1
        while span < win:
            sh = span * batch
            col = col[sh:, :] + col[:-sh, :]
            off += sh
            span *= 2
        wsum = col[hist_rows - off:hist_rows - off + rows, :]
        head = 1.0 / jnp.minimum(t_idx[:hist_rows] + 1, win).astype(F32)
        inv = jnp.concatenate([head, jnp.full((rows - hist_rows, LANES), 1.0 / win, F32)], axis=0)
        pooled.append(wsum * inv - zp[:, gi * LANES:(gi + 1) * LANES])
    pooled = jnp.concatenate(pooled, axis=1).astype(BF16)
    yp = jnp.concatenate(
        [jnp.dot(pooled[:, k * TILE_PAIR:(k + 1) * TILE_PAIR], wpool_ref[k],
                 preferred_element_type=F32) for k in range(2)], axis=1)
    yp = yp * pscale_ref[...]

    mix = jnp.concatenate([y, yp], axis=1).astype(BF16)
    o_ref[...] = h + jnp.dot(mix, wout_ref[...], preferred_element_type=F32)


def _mixer(h, layer, g, win, bmat, lam, cmat, kmat, dskip, wglu, wpool, pscale, wout, batch):
    n = h.shape[0]
    rows = MIX_STEPS * batch
    return pl.pallas_call(
        functools.partial(_mixer_kernel, batch=batch),
        grid=(n // rows,),
        in_specs=[
            _row_spec(rows, D_MODEL),
            _layer_spec((1, D_MODEL), layer),
            _layer_spec((D_MODEL, D_MODEL), layer),
            _layer_spec((N_UNITS, SSM_BLOCK * UNIT_CH, UNIT_LANES), layer),
            _layer_spec((1, STATE_LANES), layer),
            _layer_spec((N_UNITS, UNIT_LANES, SSM_BLOCK * UNIT_CH), layer),
            _layer_spec((N_UNITS, SSM_BLOCK * UNIT_CH, SSM_BLOCK * UNIT_CH), layer),
            _layer_spec((1, SSM_WIDTH), layer),
            _layer_spec((SSM_WIDTH, SSM_WIDTH), layer),
            _layer_spec((2, TILE_PAIR, TILE_PAIR), layer),
            _layer_spec((1, POOL_WIDTH), layer),
            _layer_spec((D_MODEL, D_MODEL), layer),
        ],
        out_specs=_row_spec(rows, D_MODEL),
        out_shape=jax.ShapeDtypeStruct((n, D_MODEL), F32),
        scratch_shapes=[
            pltpu.VMEM((rows // SSM_BLOCK, STATE_LANES), F32),
            pltpu.VMEM((rows // SSM_BLOCK, STATE_LANES), BF16),
            pltpu.VMEM((batch, STATE_LANES), F32),
            pltpu.VMEM((POOL_HIST_STEPS * batch, POOL_WIDTH), F32),
        ],
        compiler_params=pltpu.CompilerParams(
            dimension_semantics=("arbitrary",), vmem_limit_bytes=VMEM_LIMIT),
        name="mixer",
    )(h, g, win, bmat, lam, cmat, kmat, dskip, wglu, wpool, pscale, wout)


def _ssm_params(lam_re, lam_im, log_dt, b_re, b_im, c_re, c_im):
    dt = jnp.exp(log_dt)[:, None]
    mag = jnp.exp(lam_re * dt)
    ab_re = mag * jnp.cos(lam_im * dt)
    ab_im = mag * jnp.sin(lam_im * dt)
    den = lam_re * lam_re + lam_im * lam_im
    f_re = ((ab_re - 1.0) * lam_re + ab_im * lam_im) / den
    f_im = (ab_im * lam_re - (ab_re - 1.0) * lam_im) / den
    bt_re = b_re.transpose(0, 2, 1)
    bt_im = b_im.transpose(0, 2, 1)
    bb_re = f_re[:, None, :] * bt_re - f_im[:, None, :] * bt_im
    bb_im = f_re[:, None, :] * bt_im + f_im[:, None, :] * bt_re

    kk = SSM_BLOCK
    ug = UNIT_CH // SSM_GROUP_CH
    bh = kk * SSM_GROUP_CH
    d = jnp.arange(kk + 1, dtype=F32)[None, :, None]
    pw_mag = jnp.exp(d * (lam_re * dt)[:, None, :])
    pw_re = pw_mag * jnp.cos(d * (lam_im * dt)[:, None, :])
    pw_im = pw_mag * jnp.sin(d * (lam_im * dt)[:, None, :])
    two = lambda a, b: jnp.concatenate([a, b], axis=-1)
    two_many = lambda xs: jnp.concatenate(xs, axis=-1)

    pin_re, pin_im = pw_re[:, kk - 1::-1], pw_im[:, kk - 1::-1]
    d_in = (two(pin_re, pin_re)[:, :, None, :] * two(bb_re, bb_im)[:, None, :, :]
            + two(-pin_im, pin_im)[:, :, None, :] * two(bb_im, bb_re)[:, None, :, :])
    pout_re, pout_im = pw_re[:, 1:], pw_im[:, 1:]
    d_out = (two(c_re, c_re)[:, None, :, :] * two(pout_re, -pout_im)[:, :, None, :]
             + two(c_im, c_im)[:, None, :, :] * two(-pout_im, -pout_re)[:, :, None, :])
    kd = jnp.einsum('gihk,gok->giho', d_in, two(c_re, -c_im), precision=lax.Precision.HIGHEST)
    zero = jnp.zeros_like(kd[:, 0])
    toep = jnp.stack([two_many([kd[:, kk - 1 - (j - i)] if j >= i else zero for j in range(kk)])
                      for i in range(kk)], axis=1)
    d_in = d_in.reshape(N_UNITS, ug, bh, 2 * SSM_STATE)
    d_out = d_out.reshape(N_UNITS, ug, bh, 2 * SSM_STATE)
    toep = toep.reshape(N_UNITS, ug, bh, bh)

    row = np.arange(kk * UNIT_CH)
    row_g = row // SSM_GROUP_CH % ug
    row_bh = row // UNIT_CH * SSM_GROUP_CH + row % SSM_GROUP_CH
    lane = np.arange(UNIT_LANES)
    lane_g = 2 * (lane // TILE_PAIR) + lane // SSM_STATE % 2
    lane_rp = (lane // LANES % 2) * SSM_STATE + lane % SSM_STATE
    grp = np.arange(ug)[:, None, None]
    lsel = jnp.asarray((row_g[None, :, None] == grp)
                       & (row_bh[None, :, None] == np.arange(bh)[None, None, :]), BF16)
    rsel = jnp.asarray((lane_g[None, None, :] == grp)
                       & (lane_rp[None, None, :] == np.arange(2 * SSM_STATE)[None, :, None]), BF16)

    def place(blocks, left, right_t):
        t = jnp.einsum('ngkl,gbl->ngkb', blocks.astype(BF16), right_t,
                       preferred_element_type=F32).astype(BF16)
        return jnp.einsum('gak,ngkb->nab', left, t, preferred_element_type=F32).astype(BF16)

    rsel_t = rsel.transpose(0, 2, 1)
    bmat = place(d_in, lsel, rsel_t)
    cmat = place(d_out, lsel, rsel_t).transpose(0, 2, 1)
    kmat = place(toep, lsel, lsel)

    lam_lanes = jnp.stack([pw_re[:, kk].reshape(SSM_GROUPS // 2, 2, SSM_STATE),
                           pw_im[:, kk].reshape(SSM_GROUPS // 2, 2, SSM_STATE)], axis=1)
    lam_lanes = lam_lanes.reshape(1, STATE_LANES)
    return lam_lanes, bmat, cmat, kmat


def _pool_pairs(w_pool):
    zeros = jnp.zeros((POOL_GROUP_CH, POOL_GROUP_CH), w_pool.dtype)
    pairs = [jnp.block([[w_pool[2 * k], zeros], [zeros, w_pool[2 * k + 1]]]) for k in range(2)]
    return jnp.stack(pairs, axis=0).astype(BF16)


def kernel(x, p, ffn1_norm, ffn1_wi, ffn1_wo, mix_norm, w_in, ssm_lambda_re, ssm_lambda_im, ssm_log_dt, ssm_b_re, ssm_b_im, ssm_c_re, ssm_c_im, ssm_d, ssm_w_glu, pool_w, pool_scale, w_out, ffn2_norm, ffn2_wi, ffn2_wo, ple_norm, ple_w_gate, ple_w_proj, final_norm):
    batch, seq, d = x.shape
    depth = p.shape[0]
    rows = lambda v: v.reshape(depth, 1, -1).astype(F32)
    bf = lambda w: w.astype(BF16)

    lam, bmat, cmat, kmat = jax.vmap(_ssm_params)(ssm_lambda_re, ssm_lambda_im, ssm_log_dt,
                                                  ssm_b_re, ssm_b_im, ssm_c_re, ssm_c_im)
    wpool = jax.vmap(_pool_pairs)(pool_w)
    mx = (rows(mix_norm), bf(w_in), bmat, lam, cmat, kmat, rows(ssm_d), bf(ssm_w_glu), wpool,
          rows(pool_scale), bf(w_out))
    gt = (p, rows(ple_norm), bf(ple_w_gate), bf(ple_w_proj), final_norm.reshape(1, -1).astype(F32))
    g1, g2 = rows(ffn1_norm), rows(ffn2_norm)
    h = x
    wi, wo = bf(ffn1_wi[0]), bf(ffn1_wo[0])
    for i in range(depth):
        last = i == depth - 1
        h, wi, wo = _ffn(h, i, batch, g1, wi, wo, x_natural=(i == 0),
                         cast=((ffn2_wi, i), (ffn2_wo, i)))
        h = _mixer(h, i, *mx, batch)
        h, *nxt = _ffn(h, i, batch, g2, wi, wo, gate_args=gt, final=last,
                       cast=() if last else ((ffn1_wi, i + 1), (ffn1_wo, i + 1)))
        if not last:
            wi, wo = nxt
    return h
```

```python
import functools
import math

import jax
import jax.numpy as jnp
import numpy as np
from jax import lax
from jax.experimental import pallas as pl
from jax.experimental.pallas import tpu as pltpu

D_MODEL = 1024
D_FF = 2816
PLE_DIM = 256
SSM_WIDTH = 512
POOL_WIDTH = 512
SSM_GROUP_CH = 16
SSM_GROUPS = 32
SSM_STATE = 64
POOL_WINDOWS = (2, 4, 8, 16)
POOL_GROUP_CH = 128
EPS = 1e-6

LANES = 128
SUBLANES = 8
BF16_ROWS = 2 * SUBLANES
STATE_LANES = 2 * SSM_GROUPS * SSM_STATE
SLAB = 1024
N_SLABS = STATE_LANES // SLAB
TILE_PAIR = 2 * LANES
SSM_BLOCK = 4
UNIT_CH = 2 * LANES // SSM_BLOCK
N_UNITS = SSM_WIDTH // UNIT_CH
UNIT_LANES = STATE_LANES // N_UNITS
POOL_HIST_STEPS = 16

FFN_ROWS = 1024
FFN_CHUNK = 256
MIX_STEPS = 64
VMEM_LIMIT = 56 * 1024 * 1024

F32 = jnp.float32
BF16 = jnp.bfloat16


def _inv_rms(x):
    return lax.rsqrt(jnp.mean(x * x, axis=-1, keepdims=True) + EPS)


def _gelu_tanh(x):
    c = math.sqrt(2.0 / math.pi)
    t = jnp.tanh(x * (c + (c * 0.044715) * (x * x)))
    return x * (0.5 + 0.5 * t)


def _layer_spec(shape, layer):
    nd = len(shape)
    return pl.BlockSpec((None,) + shape, lambda i: (layer,) + (0,) * nd,
                        pipeline_mode=pl.Buffered(1))


def _const_spec(shape):
    nd = len(shape)
    return pl.BlockSpec(shape, lambda i: (0,) * nd, pipeline_mode=pl.Buffered(1))


def _row_spec(rows, cols):
    return pl.BlockSpec((rows, cols), lambda i: (i, 0))


def _slab_pitch(steps):
    return steps + SUBLANES


def _to_time_major(src_ref, slab_ref):
    batch, steps, cols = src_ref.shape
    n_slab = cols // LANES
    pitch = _slab_pitch(steps)
    for b in range(batch):
        for c in range(n_slab):
            slab_ref[c, b * pitch:b * pitch + steps, :] = src_ref[b, :, c * LANES:(c + 1) * LANES]
    return jnp.concatenate(
        [jnp.concatenate([slab_ref[c, pl.ds(t, batch, stride=pitch), :] for c in range(n_slab)],
                         axis=1) for t in range(steps)], axis=0)


def _from_time_major(y, slab_ref, dst_ref):
    batch, steps, cols = dst_ref.shape
    n_slab = cols // LANES
    pitch = _slab_pitch(steps)
    for t in range(steps):
        for c in range(n_slab):
            slab_ref[c, pl.ds(t, batch, stride=pitch), :] = (
                y[t * batch:(t + 1) * batch, c * LANES:(c + 1) * LANES])
    for b in range(batch):
        for c in range(n_slab):
            dst_ref[b, :, c * LANES:(c + 1) * LANES] = slab_ref[c, b * pitch:b * pitch + steps, :]


def _ffn_kernel(*refs, gated, final, x_natural, n_cast):
    refs = list(refs)
    slab_ref = refs.pop() if (gated or x_natural) else None
    a_ref = refs.pop()
    xg_ref = refs.pop()
    cast_dst = [refs.pop() for _ in range(n_cast)][::-1]
    o_ref = refs.pop()
    cast_src = [refs.pop() for _ in range(n_cast)][::-1]
    for src_ref, dst_ref in zip(cast_src, cast_dst):
        dst_ref[...] = src_ref[...].astype(BF16)
    x_ref, g_ref, wi_ref, wo_ref = refs[:4]
    x = _to_time_major(x_ref, slab_ref) if x_natural else x_ref[...]
    xg_ref[...] = (x * g_ref[...]).astype(BF16)
    r = _inv_rms(x)
    half_r = 0.5 * r
    for k in range(D_FF // FFN_CHUNK):
        c0 = k * FFN_CHUNK
        xg = xg_ref[...]
        hg = half_r * jnp.dot(xg, wi_ref[:, c0:c0 + FFN_CHUNK], preferred_element_type=F32)
        u = r * jnp.dot(xg, wi_ref[:, D_FF + c0:D_FF + c0 + FFN_CHUNK], preferred_element_type=F32)
        a_ref[:, c0:c0 + FFN_CHUNK] = ((hg + hg * jnp.tanh(hg)) * u).astype(BF16)
    res = x if x_natural else x_ref[...]
    y = res + 0.5 * jnp.dot(a_ref[...], wo_ref[...], preferred_element_type=F32)
    if gated:
        p_ref, pg_ref, wg_ref, wp_ref, fg_ref = refs[4:]
        pe = _to_time_major(p_ref, slab_ref).astype(BF16)
        v = _inv_rms(y) * jnp.dot(
            (y * pg_ref[...]).astype(BF16), wg_ref[...], preferred_element_type=F32)
        gate = 0.5 + 0.5 * jnp.tanh(0.5 * v)
        y = y + gate * jnp.dot(pe, wp_ref[...], preferred_element_type=F32)
    if final:
        y = y * _inv_rms(y) * fg_ref[...]
        _from_time_major(y, slab_ref, o_ref)
    else:
        o_ref[...] = y


def _cast_specs(stacked, layer, n_steps):
    _, n_rows, n_cols = stacked.shape
    block_rows = next(r for r in range(BF16_ROWS, n_rows + 1, BF16_ROWS)
                      if n_rows % r == 0 and n_rows // r <= n_steps)
    last = n_rows // block_rows - 1
    src = pl.BlockSpec((None, block_rows, n_cols), lambda i: (layer, jnp.minimum(i, last), 0))
    dst = pl.BlockSpec((block_rows, n_cols), lambda i: (jnp.minimum(i, last), 0))
    return src, dst, jax.ShapeDtypeStruct((n_rows, n_cols), BF16)


def _ffn(h, layer, batch, g, wi, wo, gate_args=None, final=False, x_natural=False, cast=()):
    steps = FFN_ROWS // batch
    natural_spec = pl.BlockSpec((batch, steps, D_MODEL), lambda i: (0, i, 0))
    if x_natural:
        n = h.shape[0] * h.shape[1]
        x_spec = natural_spec
    else:
        n = h.shape[0]
        x_spec = _row_spec(FFN_ROWS, D_MODEL)
    n_steps = n // FFN_ROWS
    in_specs = [
        x_spec,
        _layer_spec((1, D_MODEL), layer),
        _const_spec((D_MODEL, 2 * D_FF)),
        _const_spec((D_FF, D_MODEL)),
    ]
    args = [h, g, wi, wo]
    scratch = [pltpu.VMEM((FFN_ROWS, D_MODEL), BF16),
               pltpu.VMEM((FFN_ROWS, D_FF), BF16)]
    gated = gate_args is not None
    assert gated or not final
    if gated:
        in_specs += [
            pl.BlockSpec((None, batch, steps, PLE_DIM), lambda i: (layer, 0, i, 0)),
            _layer_spec((1, D_MODEL), layer),
            _layer_spec((D_MODEL, D_MODEL), layer),
            _layer_spec((PLE_DIM, D_MODEL), layer),
            _const_spec((1, D_MODEL)),
        ]
        args += list(gate_args)
    if gated or x_natural:
        n_slab = (D_MODEL if (final or x_natural) else PLE_DIM) // LANES
        scratch.append(pltpu.VMEM((n_slab, batch * _slab_pitch(steps), LANES), F32))
    if final:
        out_specs = [natural_spec]
        out_shape = [jax.ShapeDtypeStruct((batch, n // batch, D_MODEL), F32)]
    else:
        out_specs = [_row_spec(FFN_ROWS, D_MODEL)]
        out_shape = [jax.ShapeDtypeStruct((n, D_MODEL), F32)]
    for stacked, src_layer in cast:
        src, dst, shape = _cast_specs(stacked, src_layer, n_steps)
        in_specs.append(src)
        args.append(stacked)
        out_specs.append(dst)
        out_shape.append(shape)
    return pl.pallas_call(
        functools.partial(_ffn_kernel, gated=gated, final=final, x_natural=x_natural,
                          n_cast=len(cast)),
        grid=(n_steps,),
        in_specs=in_specs,
        out_specs=out_specs,
        out_shape=out_shape,
        scratch_shapes=scratch,
        compiler_params=pltpu.CompilerParams(
            dimension_semantics=("arbitrary",), vmem_limit_bytes=VMEM_LIMIT),
        name="ffn_gate" if gated else "ffn",
    )(*args)


def _mixer_kernel(h_ref, g_ref, win_ref, bmat_ref, lam_ref, cmat_ref, kmat_ref, dskip_ref, wglu_ref,
                  wpool_ref, pscale_ref, wout_ref, o_ref,
                  bu_ref, st_ref, carry_ref, hist_ref, *, batch):
    rows = h_ref.shape[0]
    steps = rows // batch
    hist_rows = POOL_HIST_STEPS * batch
    chunk = pl.program_id(0)

    @pl.when(chunk == 0)
    def _():
        carry_ref[...] = jnp.zeros_like(carry_ref)
        hist_ref[...] = jnp.zeros_like(hist_ref)

    h = h_ref[...]
    z = _inv_rms(h) * jnp.dot((h * g_ref[...]).astype(BF16), win_ref[...],
                              preferred_element_type=F32)
    u = z[:, :SSM_WIDTH]
    zp = z[:, SSM_WIDTH:]

    blocks = steps // SSM_BLOCK
    brows = blocks * batch
    u_step = [jnp.concatenate([u[(SSM_BLOCK * m + i) * batch:(SSM_BLOCK * m + i + 1) * batch]
                               for m in range(blocks)], axis=0) for i in range(SSM_BLOCK)]
    low_half = lax.broadcasted_iota(jnp.int32, (brows, LANES), 1) < UNIT_CH

    def halves(first, second, keep_low):
        a = first if keep_low else pltpu.roll(first, UNIT_CH, axis=1)
        b = pltpu.roll(second, UNIT_CH, axis=1) if keep_low else second
        return jnp.where(low_half, a, b)

    lhs = []
    for n in range(N_UNITS):
        q, low = n // 2, n % 2 == 0
        cols = [halves(u_step[2 * d][:, q * LANES:(q + 1) * LANES],
                       u_step[2 * d + 1][:, q * LANES:(q + 1) * LANES], low)
                for d in range(SSM_BLOCK // 2)]
        lhs.append(jnp.concatenate(cols, axis=1).astype(BF16))
        bu_ref[:, n * UNIT_LANES:(n + 1) * UNIT_LANES] = jnp.dot(
            lhs[n], bmat_ref[n], preferred_element_type=F32)

    for q in range(N_SLABS):
        n_tp = SLAB // TILE_PAIR
        a_re = [lam_ref[:, q * SLAB + j * TILE_PAIR:q * SLAB + j * TILE_PAIR + LANES]
                for j in range(n_tp)]
        a_im = [lam_ref[:, q * SLAB + j * TILE_PAIR + LANES:q * SLAB + (j + 1) * TILE_PAIR]
                for j in range(n_tp)]
        init = carry_ref[:, q * SLAB:(q + 1) * SLAB]
        s = [init[:, i * LANES:(i + 1) * LANES] for i in range(2 * n_tp)]
        for m in range(blocks):
            r0 = m * batch
            st_ref[r0:r0 + batch, q * SLAB:(q + 1) * SLAB] = jnp.concatenate(s, axis=1).astype(BF16)
            bu = bu_ref[r0:r0 + batch, q * SLAB:(q + 1) * SLAB]
            new = []
            for j in range(n_tp):
                s_re, s_im = s[2 * j], s[2 * j + 1]
                b_re = bu[:, j * TILE_PAIR:j * TILE_PAIR + LANES]
                b_im = bu[:, j * TILE_PAIR + LANES:(j + 1) * TILE_PAIR]
                new.append(a_re[j] * s_re - a_im[j] * s_im + b_re)
                new.append(a_re[j] * s_im + a_im[j] * s_re + b_im)
            s = new
        carry_ref[:, q * SLAB:(q + 1) * SLAB] = jnp.concatenate(s, axis=1)

    yb = [jnp.dot(st_ref[:, n * UNIT_LANES:(n + 1) * UNIT_LANES], cmat_ref[n],
                  preferred_element_type=F32)
          + jnp.dot(lhs[n], kmat_ref[n], preferred_element_type=F32) for n in range(N_UNITS)]
    y_step = []
    for j in range(SSM_BLOCK):
        d, low = j // 2, j % 2 == 0
        y_step.append(jnp.concatenate(
            [halves(yb[2 * q][:, d * LANES:(d + 1) * LANES],
                    yb[2 * q + 1][:, d * LANES:(d + 1) * LANES], low)
             for q in range(SSM_WIDTH // LANES)], axis=1))
    y = jnp.concatenate([y_step[j][m * batch:(m + 1) * batch]
                         for m in range(blocks) for j in range(SSM_BLOCK)], axis=0)

    y = y + dskip_ref[...] * u
    y = _gelu_tanh(y)
    half_y = 0.5 * y
    y = half_y + half_y * jnp.tanh(
        0.5 * jnp.dot(y.astype(BF16), wglu_ref[...], preferred_element_type=F32))

    ext = jnp.concatenate([hist_ref[...], zp], axis=0)
    hist_ref[...] = zp[rows - hist_rows:, :]
    t_idx = chunk * steps + lax.shift_right_logical(
        lax.broadcasted_iota(jnp.int32, (rows, LANES), 0), int(math.log2(batch)))
    pooled = []
    for gi, win in enumerate(POOL_WINDOWS):
        col = ext[:, gi * LANES:(gi + 1) * LANES]
        off = 0
        span = 1
        while span < win:
            sh = span * batch
            col = col[sh:, :] + col[:-sh, :]
            off += sh
            span *= 2
        wsum = col[hist_rows - off:hist_rows - off + rows, :]
        head = 1.0 / jnp.minimum(t_idx[:hist_rows] + 1, win).astype(F32)
        inv = jnp.concatenate([head, jnp.full((rows - hist_rows, LANES), 1.0 / win, F32)], axis=0)
        pooled.append(wsum * inv - zp[:, gi * LANES:(gi + 1) * LANES])
    pooled = jnp.concatenate(pooled, axis=1).astype(BF16)
    yp = jnp.concatenate(
        [jnp.dot(pooled[:, k * TILE_PAIR:(k + 1) * TILE_PAIR], wpool_ref[k],
                 preferred_element_type=F32) for k in range(2)], axis=1)
    yp = yp * pscale_ref[...]

    mix = jnp.concatenate([y, yp], axis=1).astype(BF16)
    o_ref[...] = h_ref[...] + jnp.dot(mix, wout_ref[...], preferred_element_type=F32)


def _mixer(h, layer, g, win, bmat, lam, cmat, kmat, dskip, wglu, wpool, pscale, wout, batch):
    n = h.shape[0]
    rows = MIX_STEPS * batch
    return pl.pallas_call(
        functools.partial(_mixer_kernel, batch=batch),
        grid=(n // rows,),
        in_specs=[
            _row_spec(rows, D_MODEL),
            _layer_spec((1, D_MODEL), layer),
            _layer_spec((D_MODEL, D_MODEL), layer),
            _layer_spec((N_UNITS, SSM_BLOCK * UNIT_CH, UNIT_LANES), layer),
            _layer_spec((1, STATE_LANES), layer),
            _layer_spec((N_UNITS, UNIT_LANES, SSM_BLOCK * UNIT_CH), layer),
            _layer_spec((N_UNITS, SSM_BLOCK * UNIT_CH, SSM_BLOCK * UNIT_CH), layer),
            _layer_spec((1, SSM_WIDTH), layer),
            _layer_spec((SSM_WIDTH, SSM_WIDTH), layer),
            _layer_spec((2, TILE_PAIR, TILE_PAIR), layer),
            _layer_spec((1, POOL_WIDTH), layer),
            _layer_spec((D_MODEL, D_MODEL), layer),
        ],
        out_specs=_row_spec(rows, D_MODEL),
        out_shape=jax.ShapeDtypeStruct((n, D_MODEL), F32),
        scratch_shapes=[
            pltpu.VMEM((rows // SSM_BLOCK, STATE_LANES), F32),
            pltpu.VMEM((rows // SSM_BLOCK, STATE_LANES), BF16),
            pltpu.VMEM((batch, STATE_LANES), F32),
            pltpu.VMEM((POOL_HIST_STEPS * batch, POOL_WIDTH), F32),
        ],
        compiler_params=pltpu.CompilerParams(
            dimension_semantics=("arbitrary",), vmem_limit_bytes=VMEM_LIMIT),
        name="mixer",
    )(h, g, win, bmat, lam, cmat, kmat, dskip, wglu, wpool, pscale, wout)


def _ssm_params(lam_re, lam_im, log_dt, b_re, b_im, c_re, c_im):
    dt = jnp.exp(log_dt)[:, None]
    mag = jnp.exp(lam_re * dt)
    ab_re = mag * jnp.cos(lam_im * dt)
    ab_im = mag * jnp.sin(lam_im * dt)
    den = lam_re * lam_re + lam_im * lam_im
    f_re = ((ab_re - 1.0) * lam_re + ab_im * lam_im) / den
    f_im = (ab_im * lam_re - (ab_re - 1.0) * lam_im) / den
    bt_re = b_re.transpose(0, 2, 1)
    bt_im = b_im.transpose(0, 2, 1)
    bb_re = f_re[:, None, :] * bt_re - f_im[:, None, :] * bt_im
    bb_im = f_re[:, None, :] * bt_im + f_im[:, None, :] * bt_re

    kk = SSM_BLOCK
    ug = UNIT_CH // SSM_GROUP_CH
    bh = kk * SSM_GROUP_CH
    d = jnp.arange(kk + 1, dtype=F32)[None, :, None]
    pw_mag = jnp.exp(d * (lam_re * dt)[:, None, :])
    pw_re = pw_mag * jnp.cos(d * (lam_im * dt)[:, None, :])
    pw_im = pw_mag * jnp.sin(d * (lam_im * dt)[:, None, :])
    two = lambda a, b: jnp.concatenate([a, b], axis=-1)
    two_many = lambda xs: jnp.concatenate(xs, axis=-1)

    pin_re, pin_im = pw_re[:, kk - 1::-1], pw_im[:, kk - 1::-1]
    d_in = (two(pin_re, pin_re)[:, :, None, :] * two(bb_re, bb_im)[:, None, :, :]
            + two(-pin_im, pin_im)[:, :, None, :] * two(bb_im, bb_re)[:, None, :, :])
    pout_re, pout_im = pw_re[:, 1:], pw_im[:, 1:]
    d_out = (two(c_re, c_re)[:, None, :, :] * two(pout_re, -pout_im)[:, :, None, :]
             + two(c_im, c_im)[:, None, :, :] * two(-pout_im, -pout_re)[:, :, None, :])
    kd = jnp.einsum('gihk,gok->giho', d_in, two(c_re, -c_im), precision=lax.Precision.HIGHEST)
    zero = jnp.zeros_like(kd[:, 0])
    toep = jnp.stack([two_many([kd[:, kk - 1 - (j - i)] if j >= i else zero for j in range(kk)])
                      for i in range(kk)], axis=1)
    d_in = d_in.reshape(N_UNITS, ug, bh, 2 * SSM_STATE)
    d_out = d_out.reshape(N_UNITS, ug, bh, 2 * SSM_STATE)
    toep = toep.reshape(N_UNITS, ug, bh, bh)

    row = np.arange(kk * UNIT_CH)
    row_g = row // SSM_GROUP_CH % ug
    row_bh = row // UNIT_CH * SSM_GROUP_CH + row % SSM_GROUP_CH
    lane = np.arange(UNIT_LANES)
    lane_g = 2 * (lane // TILE_PAIR) + lane // SSM_STATE % 2
    lane_rp = (lane // LANES % 2) * SSM_STATE + lane % SSM_STATE
    grp = np.arange(ug)[:, None, None]
    lsel = jnp.asarray((row_g[None, :, None] == grp)
                       & (row_bh[None, :, None] == np.arange(bh)[None, None, :]), BF16)
    rsel = jnp.asarray((lane_g[None, None, :] == grp)
                       & (lane_rp[None, None, :] == np.arange(2 * SSM_STATE)[None, :, None]), BF16)

    def place(blocks, left, right_t):
        t = jnp.einsum('ngkl,gbl->ngkb', blocks.astype(BF16), right_t,
                       preferred_element_type=F32).astype(BF16)
        return jnp.einsum('gak,ngkb->nab', left, t, preferred_element_type=F32).astype(BF16)

    rsel_t = rsel.transpose(0, 2, 1)
    bmat = place(d_in, lsel, rsel_t)
    cmat = place(d_out, lsel, rsel_t).transpose(0, 2, 1)
    kmat = place(toep, lsel, lsel)

    lam_lanes = jnp.stack([pw_re[:, kk].reshape(SSM_GROUPS // 2, 2, SSM_STATE),
                           pw_im[:, kk].reshape(SSM_GROUPS // 2, 2, SSM_STATE)], axis=1)
    lam_lanes = lam_lanes.reshape(1, STATE_LANES)
    return lam_lanes, bmat, cmat, kmat


def _pool_pairs(w_pool):
    zeros = jnp.zeros((POOL_GROUP_CH, POOL_GROUP_CH), w_pool.dtype)
    pairs = [jnp.block([[w_pool[2 * k], zeros], [zeros, w_pool[2 * k + 1]]]) for k in range(2)]
    return jnp.stack(pairs, axis=0).astype(BF16)


def kernel(x, p, ffn1_norm, ffn1_wi, ffn1_wo, mix_norm, w_in, ssm_lambda_re, ssm_lambda_im, ssm_log_dt, ssm_b_re, ssm_b_im, ssm_c_re, ssm_c_im, ssm_d, ssm_w_glu, pool_w, pool_scale, w_out, ffn2_norm, ffn2_wi, ffn2_wo, ple_norm, ple_w_gate, ple_w_proj, final_norm):
    batch, seq, d = x.shape
    depth = p.shape[0]
    rows = lambda v: v.reshape(depth, 1, -1).astype(F32)
    bf = lambda w: w.astype(BF16)

    lam, bmat, cmat, kmat = jax.vmap(_ssm_params)(ssm_lambda_re, ssm_lambda_im, ssm_log_dt,
                                                  ssm_b_re, ssm_b_im, ssm_c_re, ssm_c_im)
    wpool = jax.vmap(_pool_pairs)(pool_w)
    mx = (rows(mix_norm), bf(w_in), bmat, lam, cmat, kmat, rows(ssm_d), bf(ssm_w_glu), wpool,
          rows(pool_scale), bf(w_out))
    gt = (p, rows(ple_norm), bf(ple_w_gate), bf(ple_w_proj), final_norm.reshape(1, -1).astype(F32))
    g1, g2 = rows(ffn1_norm), rows(ffn2_norm)
    h = x
    wi, wo = bf(ffn1_wi[0]), bf(ffn1_wo[0])
    for i in range(depth):
        last = i == depth - 1
        h, wi, wo = _ffn(h, i, batch, g1, wi, wo, x_natural=(i == 0),
                         cast=((ffn2_wi, i), (ffn2_wo, i)))
        h = _mixer(h, i, *mx, batch)
        h, *nxt = _ffn(h, i, batch, g2, wi, wo, gate_args=gt, final=last,
                       cast=() if last else ((ffn1_wi, i + 1), (ffn1_wo, i + 1)))
        if not last:
            wi, wo = nxt
    return h
```
